```python
import math
import jax, jax.numpy as jnp
from jax import lax
import numpy as np

D_MODEL = 4096
BATCH = 4
SEQ = 4096
DEPTH = 1

D_HEAD = 128
D_ATTN = D_MODEL // 2
N_ATTN_HEADS = D_ATTN // D_HEAD
D_RNN = D_MODEL - D_ATTN
N_RNN_BLOCKS = 16
RNN_BLOCK = D_RNN // N_RNN_BLOCKS
D_MIX = D_ATTN + D_RNN
D_IN = 3 * D_ATTN + 2 * D_RNN
CONV_WIDTH = 4
RGLRU_C = 8.0
D_FF = int(math.ceil(8 * D_MODEL / (3 * 256))) * 256
PLE_DIM = 256
SB_BLOCK = 128
NORM_EPS = 1e-6

kernel_name = "hymba_style_stickbreak_rglru_layer"


def rms_norm(x, g):
    xf = x.astype(jnp.float32)
    y = xf * lax.rsqrt(jnp.mean(xf * xf, axis=-1, keepdims=True) + NORM_EPS)
    return (y * g.astype(jnp.float32)).astype(x.dtype)


def stick_breaking_attention(q, k, v):
    S = q.shape[1]
    dh = q.shape[-1]
    scale = dh ** -0.5
    outs = []
    for blk in range(S // SB_BLOCK):
        q0 = blk * SB_BLOCK
        kv_len = q0 + SB_BLOCK
        qb = q[:, q0:kv_len]
        kb = k[:, :kv_len]
        vb = v[:, :kv_len]
        z = jnp.einsum("bqhd,bkhd->bhqk", qb, kb).astype(jnp.float32) * scale
        t_idx = q0 + jnp.arange(SB_BLOCK)[:, None]
        s_idx = jnp.arange(kv_len)[None, :]
        causal = s_idx < t_idx
        log_keep = jnp.where(causal, -jax.nn.softplus(z), 0.0)
        after = lax.cumsum(log_keep, axis=3, reverse=True) - log_keep
        w = jnp.where(causal, jnp.exp(jax.nn.log_sigmoid(z) + after), 0.0)
        outs.append(jnp.einsum("bhqk,bkhd->bqhd", w.astype(vb.dtype), vb))
    return jnp.concatenate(outs, axis=1)


def causal_depthwise_conv(x, w, b):
    c = x.shape[-1]
    y = lax.conv_general_dilated(
        x, w[:, None, :].astype(x.dtype), window_strides=(1,),
        padding=[(CONV_WIDTH - 1, 0)],
        dimension_numbers=("NWC", "WIO", "NWC"),
        feature_group_count=c)
    return y + b.astype(x.dtype)


def rg_lru(x, w_a, b_a, w_x, b_x, lam):
    B, S, C = x.shape
    xb = x.reshape(B, S, N_RNN_BLOCKS, RNN_BLOCK)
    r = jax.nn.sigmoid(jnp.einsum("bsni,nij->bsnj", xb, w_a).reshape(B, S, C).astype(jnp.float32)
                       + b_a.astype(jnp.float32))
    i = jax.nn.sigmoid(jnp.einsum("bsni,nij->bsnj", xb, w_x).reshape(B, S, C).astype(jnp.float32)
                       + b_x.astype(jnp.float32))
    log_a = -RGLRU_C * r * jax.nn.softplus(-lam.astype(jnp.float32))
    a = jnp.exp(log_a)
    mult = jnp.sqrt(-jnp.expm1(2.0 * log_a))
    u = mult * (i * x.astype(jnp.float32))

    def combine(left, right):
        a1, b1 = left
        a2, b2 = right
        return a1 * a2, a2 * b1 + b2

    _, h = lax.associative_scan(combine, (a, u), axis=1)
    return h.astype(x.dtype)


def setup_inputs(seed: int = 0) -> dict:
    key = jax.random.key(seed)
    ks = jax.random.split(key, 24)
    f32 = jnp.float32

    def nrm(k, shape, fan_in):
        return jax.random.normal(k, shape, f32) * (fan_in ** -0.5)

    def gain(k, shape):
        return 1.0 + 0.02 * jax.random.normal(k, shape, f32)

    def bias(k, shape):
        return 0.02 * jax.random.normal(k, shape, f32)

    x = jax.random.normal(ks[0], (BATCH, SEQ, D_MODEL), f32)
    p = jax.random.normal(ks[1], (DEPTH, BATCH, SEQ, PLE_DIM), f32)
    a0 = jax.random.uniform(ks[9], (DEPTH, D_RNN), f32, 0.9, 0.999)
    base = a0 ** (1.0 / RGLRU_C)
    rg_lambda = jnp.log(base) - jnp.log1p(-base)
    return {
        "x": x,
        "p": p,
        "g_mix": gain(ks[2], (DEPTH, D_MODEL)),
        "w_in": nrm(ks[3], (DEPTH, D_MODEL, D_IN), D_MODEL),
        "conv_w": nrm(ks[4], (DEPTH, CONV_WIDTH, D_RNN), CONV_WIDTH),
        "conv_b": bias(ks[5], (DEPTH, D_RNN)),
        "w_rg_a": nrm(ks[6], (DEPTH, N_RNN_BLOCKS, RNN_BLOCK, RNN_BLOCK), RNN_BLOCK),
        "b_rg_a": bias(ks[7], (DEPTH, D_RNN)),
        "w_rg_x": nrm(ks[8], (DEPTH, N_RNN_BLOCKS, RNN_BLOCK, RNN_BLOCK), RNN_BLOCK),
        "b_rg_x": bias(ks[10], (DEPTH, D_RNN)),
        "rg_lambda": rg_lambda,
        "g_attn_out": gain(ks[11], (DEPTH, D_ATTN)),
        "g_rnn_out": gain(ks[12], (DEPTH, D_RNN)),
        "w_out": nrm(ks[13], (DEPTH, D_MIX, D_MODEL), D_MIX),
        "g_ffn": gain(ks[14], (DEPTH, D_MODEL)),
        "w_ffn_gate": nrm(ks[15], (DEPTH, D_MODEL, D_FF), D_MODEL),
        "w_ffn_up": nrm(ks[16], (DEPTH, D_MODEL, D_FF), D_MODEL),
        "w_ffn_down": nrm(ks[17], (DEPTH, D_FF, D_MODEL), D_FF),
        "g_ple": gain(ks[18], (DEPTH, D_MODEL)),
        "w_ple_gate": nrm(ks[19], (DEPTH, D_MODEL, D_MODEL), D_MODEL),
        "w_ple_proj": nrm(ks[20], (DEPTH, PLE_DIM, D_MODEL), PLE_DIM),
        "g_ple_out": gain(ks[21], (DEPTH, D_MODEL)),
        "g_final": gain(ks[22], (D_MODEL,)),
    }


def reference(x, p, g_mix, w_in, conv_w, conv_b, w_rg_a, b_rg_a, w_rg_x, b_rg_x,
              rg_lambda, g_attn_out, g_rnn_out, w_out, g_ffn, w_ffn_gate, w_ffn_up,
              w_ffn_down, g_ple, w_ple_gate, w_ple_proj, g_ple_out, g_final):
    B, S, _ = x.shape
    split_pts = [D_ATTN, 2 * D_ATTN, 3 * D_ATTN, 3 * D_ATTN + D_RNN]
    h = x
    for l in range(DEPTH):
        u = rms_norm(h, g_mix[l])
        proj = u @ w_in[l]
        q, k, v, xr, gr = jnp.split(proj, split_pts, axis=-1)
        q = q.reshape(B, S, N_ATTN_HEADS, D_HEAD)
        k = k.reshape(B, S, N_ATTN_HEADS, D_HEAD)
        v = v.reshape(B, S, N_ATTN_HEADS, D_HEAD)
        attn = stick_breaking_attention(q, k, v).reshape(B, S, D_ATTN)

        xr = causal_depthwise_conv(xr, conv_w[l], conv_b[l])
        rec = rg_lru(xr, w_rg_a[l], b_rg_a[l], w_rg_x[l], b_rg_x[l], rg_lambda[l])
        rec = rec * jax.nn.gelu(gr, approximate=True)

        mixed = jnp.concatenate([rms_norm(attn, g_attn_out[l]),
                                 rms_norm(rec, g_rnn_out[l])], axis=-1)
        h = h + mixed @ w_out[l]

        f = rms_norm(h, g_ffn[l])
        h = h + (jax.nn.silu(f @ w_ffn_gate[l]) * (f @ w_ffn_up[l])) @ w_ffn_down[l]

        pe = rms_norm(p[l] @ w_ple_proj[l], g_ple_out[l])
        gate = jax.nn.sigmoid(rms_norm(h, g_ple[l]) @ w_ple_gate[l])
        h = h + gate * pe
    return rms_norm(h, g_final)
```

```python
import functools

import jax
import jax.numpy as jnp
from jax import lax
from jax.experimental import pallas as pl
from jax.experimental.pallas import tpu as pltpu

F32 = jnp.float32
BF16 = jnp.bfloat16

NORM_EPS = 1e-6
D_HEAD = 128
RGLRU_C = 8.0

LANES = 128
SUBLANES = 8
VMEM_LIMIT_BYTES = 56 * 1024 * 1024

ROW_TILE = 1024
COL_TILE = 512
NORM_ROWS = 256
ATTN_ROWS = 256
ATTN_HEADS = 2
SCAN_ROWS = 256
FF_PAD = 512


def _params(*semantics):
    return pltpu.CompilerParams(dimension_semantics=semantics,
                                vmem_limit_bytes=VMEM_LIMIT_BYTES)


def _tile(n, want):
    t = min(n, want)
    assert n % t == 0, (n, want)
    return t


def _rms(x, g):
    y = x * lax.rsqrt(jnp.mean(x * x, axis=-1, keepdims=True) + NORM_EPS)
    return y * g


def _rmsnorm_kernel(x_ref, g_ref, o_ref):
    o_ref[...] = _rms(x_ref[...], g_ref[...]).astype(o_ref.dtype)


def rmsnorm(x, g, out_dtype):
    m, d = x.shape
    tm = _tile(m, NORM_ROWS)
    return pl.pallas_call(
        _rmsnorm_kernel,
        grid=(m // tm,),
        in_specs=[pl.BlockSpec((tm, d), lambda i: (i, 0)),
                  pl.BlockSpec((1, d), lambda i: (0, 0))],
        out_specs=pl.BlockSpec((tm, d), lambda i: (i, 0)),
        out_shape=jax.ShapeDtypeStruct((m, d), out_dtype),
        compiler_params=_params("parallel"),
        name="rmsnorm",
    )(x, g.reshape(1, d))


def _mm_kernel(a_ref, w_ref, o_ref):
    o_ref[...] = jnp.dot(a_ref[...], w_ref[...],
                         preferred_element_type=F32).astype(o_ref.dtype)


def _mm_res_kernel(a_ref, w_ref, r_ref, o_ref):
    o_ref[...] = r_ref[...] + jnp.dot(a_ref[...], w_ref[...], preferred_element_type=F32)


def matmul(a, w, out_dtype, residual=None, name="matmul"):
    m, k = a.shape
    n = w.shape[1]
    tm, tn = _tile(m, ROW_TILE), _tile(n, COL_TILE)
    in_specs = [pl.BlockSpec((tm, k), lambda i, j: (i, 0)),
                pl.BlockSpec((k, tn), lambda i, j: (0, j))]
    args = [a, w]
    body = _mm_kernel
    if residual is not None:
        in_specs.append(pl.BlockSpec((tm, tn), lambda i, j: (i, j)))
        args.append(residual)
        body = _mm_res_kernel
    return pl.pallas_call(
        body,
        grid=(m // tm, n // tn),
        in_specs=in_specs,
        out_specs=pl.BlockSpec((tm, tn), lambda i, j: (i, j)),
        out_shape=jax.ShapeDtypeStruct((m, n), out_dtype),
        compiler_params=_params("parallel", "arbitrary"),
        name=name,
    )(*args)


def _glu_kernel(a_ref, wg_ref, wu_ref, o_ref):
    a = a_ref[...]
    g = jnp.dot(a, wg_ref[...], preferred_element_type=F32)
    u = jnp.dot(a, wu_ref[...], preferred_element_type=F32)
    o_ref[...] = (g * jax.nn.sigmoid(g) * u).astype(o_ref.dtype)


def glu_matmul(a, wg, wu):
    m, k = a.shape
    n = wg.shape[1]
    tm, tn = _tile(m, ROW_TILE), _tile(n, COL_TILE)
    return pl.pallas_call(
        _glu_kernel,
        grid=(m // tm, n // tn),
        in_specs=[pl.BlockSpec((tm, k), lambda i, j: (i, 0)),
                  pl.BlockSpec((k, tn), lambda i, j: (0, j)),
                  pl.BlockSpec((k, tn), lambda i, j: (0, j))],
        out_specs=pl.BlockSpec((tm, tn), lambda i, j: (i, j)),
        out_shape=jax.ShapeDtypeStruct((m, n), BF16),
        compiler_params=_params("parallel", "arbitrary"),
        name="ffn_gate_up",
    )(a, wg, wu)


def _mm_acc_res_kernel(a_ref, w_ref, r_ref, o_ref):
    @pl.when(pl.program_id(2) == 0)
    def _():
        o_ref[...] = r_ref[...]

    o_ref[...] += jnp.dot(a_ref[...], w_ref[...], preferred_element_type=F32)


def matmul_ksplit_residual(a, w, residual, tk, name):
    m, k = a.shape
    n = w.shape[1]
    tm, tn = _tile(m, ROW_TILE), _tile(n, COL_TILE)
    assert k % tk == 0
    return pl.pallas_call(
        _mm_acc_res_kernel,
        grid=(m // tm, n // tn, k // tk),
        in_specs=[pl.BlockSpec((tm, tk), lambda i, j, kk: (i, kk)),
                  pl.BlockSpec((tk, tn), lambda i, j, kk: (kk, j)),
                  pl.BlockSpec((tm, tn), lambda i, j, kk: (i, j))],
        out_specs=pl.BlockSpec((tm, tn), lambda i, j, kk: (i, j)),
        out_shape=jax.ShapeDtypeStruct((m, n), F32),
        compiler_params=_params("parallel", "parallel", "arbitrary"),
        name=name,
    )(a, w, residual)


def _gate_kernel(a_ref, w_ref, h_ref, pe_ref, o_ref):
    z = jnp.dot(a_ref[...], w_ref[...], preferred_element_type=F32)
    o_ref[...] = h_ref[...] + jax.nn.sigmoid(z) * pe_ref[...]


def gated_embedding_add(a, w, h, pe):
    m, k = a.shape
    n = w.shape[1]
    tm, tn = _tile(m, ROW_TILE), _tile(n, COL_TILE)
    return pl.pallas_call(
        _gate_kernel,
        grid=(m // tm, n // tn),
        in_specs=[pl.BlockSpec((tm, k), lambda i, j: (i, 0)),
                  pl.BlockSpec((k, tn), lambda i, j: (0, j)),
                  pl.BlockSpec((tm, tn), lambda i, j: (i, j)),
                  pl.BlockSpec((tm, tn), lambda i, j: (i, j))],
        out_specs=pl.BlockSpec((tm, tn), lambda i, j: (i, j)),
        out_shape=jax.ShapeDtypeStruct((m, n), F32),
        compiler_params=_params("parallel", "arbitrary"),
        name="ple_gate",
    )(a, w, h, pe)


def _proj_norm_kernel(p_ref, w_ref, g_ref, o_ref):
    y = jnp.dot(p_ref[...], w_ref[...], preferred_element_type=F32)
    o_ref[...] = _rms(y, g_ref[...])


def project_rmsnorm(p, w, g):
    m, k = p.shape
    n = w.shape[1]
    tm = _tile(m, NORM_ROWS)
    return pl.pallas_call(
        _proj_norm_kernel,
        grid=(m // tm,),
        in_specs=[pl.BlockSpec((tm, k), lambda i: (i, 0)),
                  pl.BlockSpec((k, n), lambda i: (0, 0)),
                  pl.BlockSpec((1, n), lambda i: (0, 0))],
        out_specs=pl.BlockSpec((tm, n), lambda i: (i, 0)),
        out_shape=jax.ShapeDtypeStruct((m, n), F32),
        compiler_params=_params("parallel"),
        name="ple_proj_norm",
    )(p, w, g.reshape(1, n))


def _suffix_sum_matrix():
    r = lax.broadcasted_iota(jnp.int32, (2 * LANES, 2 * LANES), 0) % LANES
    c = lax.broadcasted_iota(jnp.int32, (2 * LANES, 2 * LANES), 1)
    return jnp.where((c >= LANES) | (r >= c), 1.0, 0.0).astype(BF16)


def _attn_kernel(q_ref, k_ref, v_ref, u_ref, o_ref, acc_ref, carry_ref, *, tq, heads, scale):
    qi = pl.program_id(2)
    groups = tq // LANES

    def chunk(h, kstart, masked):
        cols = slice(h * D_HEAD, (h + 1) * D_HEAD)
        q = q_ref[:, cols]
        kc = k_ref[pl.ds(kstart, tq), cols]
        vc = v_ref[pl.ds(kstart, tq), cols]
        z = lax.dot_general(q, kc, (((1,), (1,)), ((), ())),
                            preferred_element_type=F32) * scale
        log_keep = jnp.minimum(-z, 0.0) - jnp.log1p(jnp.exp(-jnp.abs(z)))
        if masked:
            row = lax.broadcasted_iota(jnp.int32, (tq, tq), 0)
            col = lax.broadcasted_iota(jnp.int32, (tq, tq), 1)
            causal = col < row
            log_keep = jnp.where(causal, log_keep, 0.0)
        carry = carry_ref[h]
        weights = [None] * groups
        for gi in reversed(range(groups)):
            lanes = slice(gi * LANES, (gi + 1) * LANES)
            lg = log_keep[:, lanes]
            hi = lg.astype(BF16)
            lo = (lg - hi.astype(F32)).astype(BF16)
            sums = jnp.dot(jnp.concatenate([hi, lo], axis=1), u_ref[...],
                           preferred_element_type=F32)
            w = jnp.exp(z[:, lanes] + carry + sums[:, :LANES])
            if masked:
                w = jnp.where(causal[:, lanes], w, 0.0)
            weights[gi] = w.astype(BF16)
            carry = carry + sums[:, LANES:]
        carry_ref[h] = carry
        acc_ref[h] += jnp.dot(jnp.concatenate(weights, axis=1), vc,
                              preferred_element_type=F32)

    for h in range(heads):
        carry_ref[h] = jnp.zeros((tq, LANES), F32)
        acc_ref[h] = jnp.zeros((tq, D_HEAD), F32)
        chunk(h, pl.multiple_of(qi * tq, tq), True)

    def body(j, c):
        kstart = pl.multiple_of((qi - 1 - j) * tq, tq)
        for h in range(heads):
            chunk(h, kstart, False)
        return c

    lax.fori_loop(0, qi, body, 0)

    for h in range(heads):
        o_ref[:, h * D_HEAD:(h + 1) * D_HEAD] = acc_ref[h]


def stick_breaking_attention(qkv, batch, seq, d_attn):
    n_heads = d_attn // D_HEAD
    heads = min(ATTN_HEADS, n_heads)
    assert n_heads % heads == 0
    width = heads * D_HEAD
    pairs = d_attn // width
    tq = _tile(seq, ATTN_ROWS)
    nq = seq // tq
    kern = functools.partial(_attn_kernel, tq=tq, heads=heads, scale=D_HEAD ** -0.5)
    return pl.pallas_call(
        kern,
        grid=(batch, pairs, nq),
        in_specs=[pl.BlockSpec((tq, width), lambda b, p, i: (b * nq + i, p)),
                  pl.BlockSpec((seq, width), lambda b, p, i: (b, pairs + p)),
                  pl.BlockSpec((seq, width), lambda b, p, i: (b, 2 * pairs + p)),
                  pl.BlockSpec((2 * LANES, 2 * LANES), lambda b, p, i: (0, 0))],
        out_specs=pl.BlockSpec((tq, width), lambda b, p, i: (b * nq + i, p)),
        out_shape=jax.ShapeDtypeStruct((batch * seq, d_attn), F32),
        scratch_shapes=[pltpu.VMEM((heads, tq, D_HEAD), F32),
                        pltpu.VMEM((heads, tq, LANES), F32)],
        compiler_params=_params("parallel", "parallel", "arbitrary"),
        name="stick_breaking_attention",
    )(qkv, qkv, qkv, _suffix_sum_matrix())


def _softplus(x):
    return jnp.maximum(x, 0.0) + jnp.log1p(jnp.exp(-jnp.abs(x)))


def _mixer_kernel(xr_ref, gr_ref, attn_ref, cw_ref, cb_ref, wg_ref, ba_ref, bx_ref, lam_ref,
                  ga_ref, gn_ref, o_ref, tail_ref, h_ref, rec_ref, *, ts, n_blocks, conv_width):
    @pl.when(pl.program_id(1) == 0)
    def _():
        tail_ref[...] = jnp.zeros_like(tail_ref)
        h_ref[...] = jnp.zeros_like(h_ref)

    row = lax.broadcasted_iota(jnp.int32, (ts, LANES), 0)
    row8 = lax.broadcasted_iota(jnp.int32, (SUBLANES, LANES), 0)
    sub = row % SUBLANES

    def block(n, ssq):
        lanes = pl.ds(pl.multiple_of(n * LANES, LANES), LANES)
        x = xr_ref[:, lanes]
        tail = tail_ref[:, lanes]
        y = cb_ref[:, lanes] + cw_ref[conv_width - 1:conv_width, lanes] * x
        for d in range(1, conv_width):
            xs = pltpu.roll(x, d, axis=0)
            head = jnp.where(row8 < d, pltpu.roll(tail, d, axis=0), xs[:SUBLANES])
            xs = jnp.concatenate([head, xs[SUBLANES:]], axis=0)
            y = y + cw_ref[conv_width - 1 - d:conv_width - d, lanes] * xs
        tail_ref[:, lanes] = x[ts - SUBLANES:]

        gates = jnp.dot(y.astype(BF16), wg_ref[n], preferred_element_type=F32)
        r = jax.nn.sigmoid(gates[:, :LANES] + ba_ref[:, lanes])
        i = jax.nn.sigmoid(gates[:, LANES:] + bx_ref[:, lanes])
        log_a = -RGLRU_C * r * _softplus(-lam_ref[:, lanes])
        a = jnp.exp(log_a)
        u = jnp.sqrt(-jnp.tanh(log_a) * (a * a + 1.0)) * (i * y)

        for d in (1, 2, 4):
            keep = sub >= d
            a_prev = jnp.where(keep, pltpu.roll(a, d, axis=0), 1.0)
            u_prev = jnp.where(keep, pltpu.roll(u, d, axis=0), 0.0)
            u = a * u_prev + u
            a = a * a_prev
        h_prev = h_ref[:, lanes]
        hs = []
        for g in range(ts // SUBLANES):
            rows = slice(g * SUBLANES, (g + 1) * SUBLANES)
            hg = a[rows] * h_prev + u[rows]
            hs.append(hg)
            h_prev = jnp.broadcast_to(hg[SUBLANES - 1:SUBLANES], (SUBLANES, LANES))
        h_ref[:, lanes] = h_prev
        rec = jnp.concatenate(hs, axis=0) * jax.nn.gelu(gr_ref[:, lanes], approximate=True)
        rec_ref[:, lanes] = rec
        return ssq + jnp.sum(rec * rec, axis=1, keepdims=True)

    ssq = lax.fori_loop(0, n_blocks, block, jnp.zeros((ts, 1), F32))
    d_attn = attn_ref.shape[1]
    d_rnn = n_blocks * LANES
    o_ref[:, :d_attn] = _rms(attn_ref[...], ga_ref[...]).astype(o_ref.dtype)
    inv = lax.rsqrt(ssq / d_rnn + NORM_EPS)
    o_ref[:, d_attn:] = (rec_ref[...] * inv * gn_ref[...]).astype(o_ref.dtype)


def mixer_epilogue(xg, attn, conv_w, conv_b, w_gates, b_a, b_x, lam, g_attn, g_rnn, batch, seq):
    d_rnn = xg.shape[1] // 2
    d_attn = attn.shape[1]
    n_blocks = d_rnn // LANES
    assert w_gates.shape == (n_blocks, LANES, 2 * LANES)
    conv_width = conv_w.shape[0]
    ts = _tile(seq, SCAN_ROWS)
    ns = seq // ts
    row1 = lambda v: v.reshape(1, -1)
    const = lambda shape: pl.BlockSpec(shape, lambda b, s: (0,) * len(shape))
    kern = functools.partial(_mixer_kernel, ts=ts, n_blocks=n_blocks, conv_width=conv_width)
    return pl.pallas_call(
        kern,
        grid=(batch, ns),
        in_specs=[pl.BlockSpec((ts, d_rnn), lambda b, s: (b * ns + s, 0)),
                  pl.BlockSpec((ts, d_rnn), lambda b, s: (b * ns + s, 1)),
                  pl.BlockSpec((ts, d_attn), lambda b, s: (b * ns + s, 0)),
                  const((conv_width, d_rnn)), const((1, d_rnn)),
                  const((n_blocks, LANES, 2 * LANES)),
                  const((1, d_rnn)), const((1, d_rnn)), const((1, d_rnn)),
                  const((1, d_attn)), const((1, d_rnn))],
        out_specs=pl.BlockSpec((ts, d_attn + d_rnn), lambda b, s: (b * ns + s, 0)),
        out_shape=jax.ShapeDtypeStruct((batch * seq, d_attn + d_rnn), BF16),
        scratch_shapes=[pltpu.VMEM((SUBLANES, d_rnn), F32),
                        pltpu.VMEM((SUBLANES, d_rnn), F32),
                        pltpu.VMEM((ts, d_rnn), F32)],
        compiler_params=_params("parallel", "arbitrary"),
        name="rglru_mixer",
    )(xg, xg, attn, conv_w, row1(conv_b), w_gates, row1(b_a), row1(b_x), row1(lam),
      row1(g_attn), row1(g_rnn))


def kernel(x, p, g_mix, w_in, conv_w, conv_b, w_rg_a, b_rg_a, w_rg_x, b_rg_x, rg_lambda,
           g_attn_out, g_rnn_out, w_out, g_ffn, w_ffn_gate, w_ffn_up, w_ffn_down, g_ple,
           w_ple_gate, w_ple_proj, g_ple_out, g_final):
    batch, seq, d_model = x.shape
    depth = w_in.shape[0]
    d_attn = g_attn_out.shape[-1]
    d_rnn = g_rnn_out.shape[-1]
    d_ff = w_ffn_gate.shape[-1]
    ff_pad = -d_ff % FF_PAD
    m = batch * seq

    h = x.reshape(m, d_model)
    for l in range(depth):
        u = rmsnorm(h, g_mix[l], BF16)
        w_qkv = w_in[l, :, :3 * d_attn].astype(BF16)
        w_rnn = w_in[l, :, 3 * d_attn:].astype(BF16)
        qkv = matmul(u, w_qkv, BF16, name="in_proj_qkv")
        xg = matmul(u, w_rnn, F32, name="in_proj_rnn")
        attn = stick_breaking_attention(qkv, batch, seq, d_attn)
        w_gates = jnp.concatenate([w_rg_a[l], w_rg_x[l]], axis=-1).astype(BF16)
        mixed = mixer_epilogue(xg, attn, conv_w[l], conv_b[l], w_gates, b_rg_a[l], b_rg_x[l],
                               rg_lambda[l], g_attn_out[l], g_rnn_out[l], batch, seq)
        h = matmul(mixed, w_out[l].astype(BF16), F32, residual=h, name="out_proj")

        f = rmsnorm(h, g_ffn[l], BF16)
        wg = jnp.pad(w_ffn_gate[l].astype(BF16), ((0, 0), (0, ff_pad)))
        wu = jnp.pad(w_ffn_up[l].astype(BF16), ((0, 0), (0, ff_pad)))
        wd = jnp.pad(w_ffn_down[l].astype(BF16), ((0, ff_pad), (0, 0)))
        hidden = glu_matmul(f, wg, wu)
        n_ksplit = 4 if (d_ff + ff_pad) % (4 * 2 * LANES) == 0 else 1
        h = matmul_ksplit_residual(hidden, wd, h, (d_ff + ff_pad) // n_ksplit, "ffn_down")

        pe = project_rmsnorm(p[l].reshape(m, -1).astype(BF16), w_ple_proj[l].astype(BF16),
                             g_ple_out[l])
        hn = rmsnorm(h, g_ple[l], BF16)
        h = gated_embedding_add(hn, w_ple_gate[l].astype(BF16), h, pe)
    out = rmsnorm(h, g_final, F32)
    return out.reshape(batch, seq, d_model)
```

```python
import functools

import jax
import jax.numpy as jnp
from jax import lax
from jax.experimental import pallas as pl
from jax.experimental.pallas import tpu as pltpu

F32 = jnp.float32
BF16 = jnp.bfloat16

NORM_EPS = 1e-6
D_HEAD = 128
RGLRU_C = 8.0
LOG2_E = 1.4426950408889634
MIN_LOG2_WEIGHT = -150.0

LANES = 128
SUBLANES = 8
VMEM_LIMIT_BYTES = 60 * 1024 * 1024

PROJ_TILES = (1024, 1024)
GLU_TILES = (2048, 256)
DOWN_TILES = (512, 512)
PLE_TILES = (512, 512)
NORM_ROWS = 256
ATTN_ROWS = 256
ATTN_HEADS = 8
SCAN_ROWS = 256


def _params(*semantics):
    return pltpu.CompilerParams(dimension_semantics=semantics,
                                vmem_limit_bytes=VMEM_LIMIT_BYTES)


def _tile(n, want):
    t = min(n, want)
    assert n % t == 0, (n, want)
    return t


def _rms(x, g):
    y = x * lax.rsqrt(jnp.mean(x * x, axis=-1, keepdims=True) + NORM_EPS)
    return y * g


def _rmsnorm_kernel(x_ref, g_ref, o_ref):
    o_ref[...] = _rms(x_ref[...], g_ref[...]).astype(o_ref.dtype)


def rmsnorm(x, g, out_dtype):
    m, d = x.shape
    tm = _tile(m, NORM_ROWS)
    return pl.pallas_call(
        _rmsnorm_kernel,
        grid=(m // tm,),
        in_specs=[pl.BlockSpec((tm, d), lambda i: (i, 0)),
                  pl.BlockSpec((1, d), lambda i: (0, 0))],
        out_specs=pl.BlockSpec((tm, d), lambda i: (i, 0)),
        out_shape=jax.ShapeDtypeStruct((m, d), out_dtype),
        compiler_params=_params("parallel"),
        name="rmsnorm",
    )(x, g.reshape(1, d))


def _mm_kernel(a_ref, w_ref, o_ref):
    o_ref[...] = jnp.dot(a_ref[...], w_ref[...],
                         preferred_element_type=F32).astype(o_ref.dtype)


def _mm_res_kernel(a_ref, w_ref, r_ref, o_ref):
    o_ref[...] = r_ref[...] + jnp.dot(a_ref[...], w_ref[...], preferred_element_type=F32)


def matmul(a, w, out_dtype, tiles, residual=None, name="matmul", col_start=0, n=None):
    m, k = a.shape
    n = w.shape[1] if n is None else n
    tm, tn = _tile(m, tiles[0]), _tile(n, tiles[1])
    assert col_start % tn == 0
    j0 = col_start // tn
    in_specs = [pl.BlockSpec((tm, k), lambda i, j: (i, 0)),
                pl.BlockSpec((k, tn), lambda i, j: (0, j0 + j))]
    args = [a, w]
    body = _mm_kernel
    if residual is not None:
        in_specs.append(pl.BlockSpec((tm, tn), lambda i, j: (i, j)))
        args.append(residual)
        body = _mm_res_kernel
    return pl.pallas_call(
        body,
        grid=(m // tm, n // tn),
        in_specs=in_specs,
        out_specs=pl.BlockSpec((tm, tn), lambda i, j: (i, j)),
        out_shape=jax.ShapeDtypeStruct((m, n), out_dtype),
        compiler_params=_params("parallel", "arbitrary"),
        name=name,
    )(*args)


def _glu_kernel(a_ref, wg_ref, wu_ref, o_ref):
    a = a_ref[...]
    g = jnp.dot(a, wg_ref[...], preferred_element_type=F32)
    u = jnp.dot(a, wu_ref[...], preferred_element_type=F32)
    o_ref[...] = (g * jax.nn.sigmoid(g) * u).astype(o_ref.dtype)


def glu_matmul(a, wg, wu):
    m, k = a.shape
    n = wg.shape[1]
    tm, tn = _tile(m, GLU_TILES[0]), _tile(n, GLU_TILES[1])
    return pl.pallas_call(
        _glu_kernel,
        grid=(m // tm, n // tn),
        in_specs=[pl.BlockSpec((tm, k), lambda i, j: (i, 0)),
                  pl.BlockSpec((k, tn), lambda i, j: (0, j)),
                  pl.BlockSpec((k, tn), lambda i, j: (0, j))],
        out_specs=pl.BlockSpec((tm, tn), lambda i, j: (i, j)),
        out_shape=jax.ShapeDtypeStruct((m, n), BF16),
        compiler_params=_params("parallel", "arbitrary"),
        name="ffn_gate_up",
    )(a, wg, wu)


def _ple_kernel(a_ref, w_ref, h_ref, p_ref, wp_ref, gp_ref, gf_ref, o_ref, pe_ref, *,
                tn, final_norm):
    j = pl.program_id(1)

    @pl.when(j == 0)
    def _():
        pe_ref[...] = _rms(jnp.dot(p_ref[...], wp_ref[...], preferred_element_type=F32),
                           gp_ref[...])

    cols = pl.ds(pl.multiple_of(j * tn, tn), tn)
    z = jnp.dot(a_ref[...], w_ref[...], preferred_element_type=F32)
    o_ref[:, cols] = h_ref[...] + jax.nn.sigmoid(z) * pe_ref[:, cols]

    if final_norm:
        @pl.when(j == pl.num_programs(1) - 1)
        def _():
            o_ref[...] = _rms(o_ref[...], gf_ref[...])


def gated_embedding_add(a, w, h, p, wp, gp, gf, final_norm):
    m, k = a.shape
    n = w.shape[1]
    kp = p.shape[1]
    tm, tn = _tile(m, PLE_TILES[0]), _tile(n, PLE_TILES[1])
    kern = functools.partial(_ple_kernel, tn=tn, final_norm=final_norm)
    return pl.pallas_call(
        kern,
        grid=(m // tm, n // tn),
        in_specs=[pl.BlockSpec((tm, k), lambda i, j: (i, 0)),
                  pl.BlockSpec((k, tn), lambda i, j: (0, j)),
                  pl.BlockSpec((tm, tn), lambda i, j: (i, j)),
                  pl.BlockSpec((tm, kp), lambda i, j: (i, 0)),
                  pl.BlockSpec((kp, n), lambda i, j: (0, 0)),
                  pl.BlockSpec((1, n), lambda i, j: (0, 0)),
                  pl.BlockSpec((1, n), lambda i, j: (0, 0))],
        out_specs=pl.BlockSpec((tm, n), lambda i, j: (i, 0)),
        out_shape=jax.ShapeDtypeStruct((m, n), F32),
        scratch_shapes=[pltpu.VMEM((tm, n), F32)],
        compiler_params=_params("parallel", "arbitrary"),
        name="ple_gate",
    )(a, w, h, p, wp, gp.reshape(1, n), gf.reshape(1, n))


def _suffix_sum_matrix(n):
    r = lax.broadcasted_iota(jnp.int32, (2 * n, n), 0) % n
    c = lax.broadcasted_iota(jnp.int32, (2 * n, n), 1)
    return jnp.where(r >= c, 1.0, 0.0).astype(BF16)


def _attn_kernel(q_ref, k_ref, v_ref, t_ref, o_ref, acc_ref, carry_ref, *, tq, heads, scale):
    qi = pl.program_id(2)
    groups = tq // LANES
    sign_bit = jnp.uint32(0x80000000)

    def all_heads(kstart, masked):
        hs = range(heads)
        cols = [slice(h * D_HEAD, (h + 1) * D_HEAD) for h in hs]
        if masked:
            row = lax.broadcasted_iota(jnp.int32, (tq, tq), 0)
            col = lax.broadcasted_iota(jnp.int32, (tq, tq), 1)
            causal = col < row
        nz = [lax.dot_general(q_ref[:, cols[h]], k_ref[pl.ds(kstart, tq), cols[h]],
                              (((1,), (1,)), ((), ())), preferred_element_type=F32)
              * (-scale * LOG2_E) for h in hs]
        sums = []
        for h in hs:
            neg_abs = lax.bitcast_convert_type(
                lax.bitcast_convert_type(nz[h], jnp.uint32) | sign_bit, F32)
            log_keep = jnp.minimum(nz[h], 0.0) - jnp.log(1.0 + jnp.exp2(neg_abs)) * LOG2_E
            if masked:
                log_keep = jnp.where(causal, log_keep, 0.0)
            hi = log_keep.astype(BF16)
            lo = (log_keep - hi.astype(F32)).astype(BF16)
            sums.append(jnp.dot(jnp.concatenate([hi, lo], axis=1), t_ref[...],
                                preferred_element_type=F32))
        top = None
        for h in hs:
            carry = carry_ref[h]
            w = jnp.exp2(sums[h] - nz[h] + jnp.concatenate([carry] * groups, axis=1))
            if masked:
                w = jnp.where(causal, w, 0.0)
            acc_ref[h] += jnp.dot(w.astype(BF16), v_ref[pl.ds(kstart, tq), cols[h]],
                                  preferred_element_type=F32)
            carry = carry + jnp.broadcast_to(sums[h][:, 0:1], (tq, LANES))
            carry_ref[h] = carry
            m = jnp.max(carry)
            top = m if top is None else jnp.maximum(top, m)
        return top

    for h in range(heads):
        carry_ref[h] = jnp.zeros((tq, LANES), F32)
        acc_ref[h] = jnp.zeros((tq, D_HEAD), F32)
    top = all_heads(pl.multiple_of(qi * tq, tq), True)

    def more(c):
        j, top = c
        return jnp.logical_and(j < qi, top >= MIN_LOG2_WEIGHT)

    def body(c):
        j, _ = c
        return j + 1, all_heads(pl.multiple_of((qi - 1 - j) * tq, tq), False)

    lax.while_loop(more, body, (jnp.int32(0), top))

    for h in range(heads):
        o_ref[:, h * D_HEAD:(h + 1) * D_HEAD] = acc_ref[h]


def stick_breaking_attention(qkv, batch, seq, d_attn):
    n_heads = d_attn // D_HEAD
    heads = min(ATTN_HEADS, n_heads)
    assert n_heads % heads == 0
    width = heads * D_HEAD
    pairs = d_attn // width
    tq = _tile(seq, ATTN_ROWS)
    nq = seq // tq
    kern = functools.partial(_attn_kernel, tq=tq, heads=heads, scale=D_HEAD ** -0.5)
    return pl.pallas_call(
        kern,
        grid=(batch, pairs, nq),
        in_specs=[pl.BlockSpec((tq, width), lambda b, p, i: (b * nq + i, p)),
                  pl.BlockSpec((seq, width), lambda b, p, i: (b, pairs + p)),
                  pl.BlockSpec((seq, width), lambda b, p, i: (b, 2 * pairs + p)),
                  pl.BlockSpec((2 * tq, tq), lambda b, p, i: (0, 0))],
        out_specs=pl.BlockSpec((tq, width), lambda b, p, i: (b * nq + i, p)),
        out_shape=jax.ShapeDtypeStruct((batch * seq, d_attn), F32),
        scratch_shapes=[pltpu.VMEM((heads, tq, D_HEAD), F32),
                        pltpu.VMEM((heads, tq, LANES), F32)],
        compiler_params=_params("parallel", "parallel", "arbitrary"),
        name="stick_breaking_attention",
    )(qkv, qkv, qkv, _suffix_sum_matrix(tq))


def _softplus(x):
    return jnp.maximum(x, 0.0) + jnp.log1p(jnp.exp(-jnp.abs(x)))


def _mixer_kernel(xr_ref, gr_ref, attn_ref, cw_ref, cb_ref, wg_ref, ba_ref, bx_ref, lam_ref,
                  ga_ref, gn_ref, o_ref, tail_ref, h_ref, rec_ref, *, ts, n_blocks, conv_width):
    @pl.when(pl.program_id(1) == 0)
    def _():
        tail_ref[...] = jnp.zeros_like(tail_ref)
        h_ref[...] = jnp.zeros_like(h_ref)

    row = lax.broadcasted_iota(jnp.int32, (ts, LANES), 0)
    row8 = lax.broadcasted_iota(jnp.int32, (SUBLANES, LANES), 0)
    sub = row % SUBLANES

    def block(n, ssq):
        lanes = pl.ds(pl.multiple_of(n * LANES, LANES), LANES)
        x = xr_ref[:, lanes]
        tail = tail_ref[:, lanes]
        y = cb_ref[:, lanes] + cw_ref[conv_width - 1:conv_width, lanes] * x
        for d in range(1, conv_width):
            xs = pltpu.roll(x, d, axis=0)
            head = jnp.where(row8 < d, pltpu.roll(tail, d, axis=0), xs[:SUBLANES])
            xs = jnp.concatenate([head, xs[SUBLANES:]], axis=0)
            y = y + cw_ref[conv_width - 1 - d:conv_width - d, lanes] * xs
        tail_ref[:, lanes] = x[ts - SUBLANES:]

        gates = jnp.dot(y.astype(BF16), wg_ref[n], preferred_element_type=F32)
        r = jax.nn.sigmoid(gates[:, :LANES] + ba_ref[:, lanes])
        i = jax.nn.sigmoid(gates[:, LANES:] + bx_ref[:, lanes])
        log_a = -RGLRU_C * r * _softplus(-lam_ref[:, lanes])
        a = jnp.exp(log_a)
        v = -jnp.tanh(log_a) * (a * a + 1.0)
        u = jnp.where(v > 0.0, v * lax.rsqrt(v), 0.0) * (i * y)

        for d in (1, 2, 4):
            keep = sub >= d
            a_prev = jnp.where(keep, pltpu.roll(a, d, axis=0), 1.0)
            u_prev = jnp.where(keep, pltpu.roll(u, d, axis=0), 0.0)
            u = a * u_prev + u
            a = a * a_prev
        h_prev = h_ref[:, lanes]
        hs = []
        for g in range(ts // SUBLANES):
            rows = slice(g * SUBLANES, (g + 1) * SUBLANES)
            hg = a[rows] * h_prev + u[rows]
            hs.append(hg)
            h_prev = jnp.broadcast_to(hg[SUBLANES - 1:SUBLANES], (SUBLANES, LANES))
        h_ref[:, lanes] = h_prev
        rec = jnp.concatenate(hs, axis=0) * jax.nn.gelu(gr_ref[:, lanes], approximate=True)
        rec_ref[:, lanes] = rec
        return ssq + jnp.sum(rec * rec, axis=1, keepdims=True)

    ssq = lax.fori_loop(0, n_blocks, block, jnp.zeros((ts, 1), F32))
    d_attn = attn_ref.shape[1]
    d_rnn = n_blocks * LANES
    o_ref[:, :d_attn] = _rms(attn_ref[...], ga_ref[...]).astype(o_ref.dtype)
    inv = lax.rsqrt(ssq / d_rnn + NORM_EPS)
    o_ref[:, d_attn:] = (rec_ref[...] * inv * gn_ref[...]).astype(o_ref.dtype)


def mixer_epilogue(xg, attn, conv_w, conv_b, w_gates, b_a, b_x, lam, g_attn, g_rnn, batch, seq):
    d_rnn = xg.shape[1] // 2
    d_attn = attn.shape[1]
    n_blocks = d_rnn // LANES
    assert w_gates.shape == (n_blocks, LANES, 2 * LANES)
    conv_width = conv_w.shape[0]
    ts = _tile(seq, SCAN_ROWS)
    ns = seq // ts
    row1 = lambda v: v.reshape(1, -1)
    const = lambda shape: pl.BlockSpec(shape, lambda b, s: (0,) * len(shape))
    kern = functools.partial(_mixer_kernel, ts=ts, n_blocks=n_blocks, conv_width=conv_width)
    return pl.pallas_call(
        kern,
        grid=(batch, ns),
        in_specs=[pl.BlockSpec((ts, d_rnn), lambda b, s: (b * ns + s, 0)),
                  pl.BlockSpec((ts, d_rnn), lambda b, s: (b * ns + s, 1)),
                  pl.BlockSpec((ts, d_attn), lambda b, s: (b * ns + s, 0)),
                  const((conv_width, d_rnn)), const((1, d_rnn)),
                  const((n_blocks, LANES, 2 * LANES)),
                  const((1, d_rnn)), const((1, d_rnn)), const((1, d_rnn)),
                  const((1, d_attn)), const((1, d_rnn))],
        out_specs=pl.BlockSpec((ts, d_attn + d_rnn), lambda b, s: (b * ns + s, 0)),
        out_shape=jax.ShapeDtypeStruct((batch * seq, d_attn + d_rnn), BF16),
        scratch_shapes=[pltpu.VMEM((SUBLANES, d_rnn), F32),
                        pltpu.VMEM((SUBLANES, d_rnn), F32),
                        pltpu.VMEM((ts, d_rnn), F32)],
        compiler_params=_params("parallel", "arbitrary"),
        name="rglru_mixer",
    )(xg, xg, attn, conv_w, row1(conv_b), w_gates, row1(b_a), row1(b_x), row1(lam),
      row1(g_attn), row1(g_rnn))


def kernel(x, p, g_mix, w_in, conv_w, conv_b, w_rg_a, b_rg_a, w_rg_x, b_rg_x, rg_lambda,
           g_attn_out, g_rnn_out, w_out, g_ffn, w_ffn_gate, w_ffn_up, w_ffn_down, g_ple,
           w_ple_gate, w_ple_proj, g_ple_out, g_final):
    batch, seq, d_model = x.shape
    depth = w_in.shape[0]
    d_attn = g_attn_out.shape[-1]
    d_rnn = g_rnn_out.shape[-1]
    m = batch * seq

    h = x.reshape(m, d_model)
    for l in range(depth):
        u = rmsnorm(h, g_mix[l], BF16)
        w_in_l = w_in[l].astype(BF16)
        qkv = matmul(u, w_in_l, BF16, PROJ_TILES, name="in_proj_qkv", n=3 * d_attn)
        xg = matmul(u, w_in_l, F32, PROJ_TILES, name="in_proj_rnn",
                    col_start=3 * d_attn, n=2 * d_rnn)
        attn = stick_breaking_attention(qkv, batch, seq, d_attn)
        w_gates = jnp.concatenate([w_rg_a[l], w_rg_x[l]], axis=-1).astype(BF16)
        mixed = mixer_epilogue(xg, attn, conv_w[l], conv_b[l], w_gates, b_rg_a[l], b_rg_x[l],
                               rg_lambda[l], g_attn_out[l], g_rnn_out[l], batch, seq)
        h = matmul(mixed, w_out[l].astype(BF16), F32, PROJ_TILES, residual=h, name="out_proj")

        f = rmsnorm(h, g_ffn[l], BF16)
        hidden = glu_matmul(f, w_ffn_gate[l].astype(BF16), w_ffn_up[l].astype(BF16))
        h = matmul(hidden, w_ffn_down[l].astype(BF16), F32, DOWN_TILES, residual=h,
                   name="ffn_down")

        hn = rmsnorm(h, g_ple[l], BF16)
        h = gated_embedding_add(hn, w_ple_gate[l].astype(BF16), h,
                                p[l].reshape(m, -1).astype(BF16), w_ple_proj[l].astype(BF16),
                                g_ple_out[l], g_final, final_norm=(l == depth - 1))
    return h.reshape(batch, seq, d_model)
```

```python
import functools

import jax
import jax.numpy as jnp
from jax import lax
from jax.experimental import pallas as pl
from jax.experimental.pallas import tpu as pltpu

F32 = jnp.float32
BF16 = jnp.bfloat16

NORM_EPS = 1e-6
D_HEAD = 128
RGLRU_C = 8.0
LOG2_E = 1.4426950408889634
MIN_LOG2_WEIGHT = -150.0

LANES = 128
SUBLANES = 8
VMEM_LIMIT_BYTES = 60 * 1024 * 1024

PROJ_TILES = (1024, 1024)
OUT_TILES = (1024, 512)
GLU_TILES = (2048, 256)
DOWN_TILES = (512, 512)
PLE_TILES = (512, 512)
NORM_ROWS = 256
ATTN_ROWS = 256
ATTN_HEADS = 8
SCAN_ROWS = 256


def _params(*semantics):
    return pltpu.CompilerParams(dimension_semantics=semantics,
                                vmem_limit_bytes=VMEM_LIMIT_BYTES)


def _tile(n, want):
    t = min(n, want)
    assert n % t == 0, (n, want)
    return t


def _rms(x, g):
    y = x * lax.rsqrt(jnp.mean(x * x, axis=-1, keepdims=True) + NORM_EPS)
    return y * g


def _rmsnorm_kernel(x_ref, g_ref, o_ref):
    o_ref[...] = _rms(x_ref[...], g_ref[...]).astype(o_ref.dtype)


def rmsnorm(x, g, out_dtype):
    m, d = x.shape
    tm = _tile(m, NORM_ROWS)
    return pl.pallas_call(
        _rmsnorm_kernel,
        grid=(m // tm,),
        in_specs=[pl.BlockSpec((tm, d), lambda i: (i, 0)),
                  pl.BlockSpec((1, d), lambda i: (0, 0))],
        out_specs=pl.BlockSpec((tm, d), lambda i: (i, 0)),
        out_shape=jax.ShapeDtypeStruct((m, d), out_dtype),
        compiler_params=_params("parallel"),
        name="rmsnorm",
    )(x, g.reshape(1, d))


def _inv_rms(ssq_ref, d):
    return lax.rsqrt(ssq_ref[:, 0:1] * (1.0 / d) + NORM_EPS)


def _emit_scaled(h, g_ref, hg_ref, ssq_ref, first):
    hg_ref[...] = (h * g_ref[...]).astype(hg_ref.dtype)
    part = jnp.broadcast_to(jnp.sum(h * h, axis=1, keepdims=True), ssq_ref.shape)

    @pl.when(first)
    def _():
        ssq_ref[...] = part

    @pl.when(jnp.logical_not(first))
    def _():
        ssq_ref[...] += part


class _Side:
    def __init__(self, w, n_steps, linear_step):
        rows, cols = w.shape
        units = rows // (2 * SUBLANES)
        assert units * 2 * SUBLANES == rows
        n_chunks = max(c for c in range(1, min(units, n_steps) + 1) if units % c == 0)
        chunk = rows // n_chunks
        index = lambda *g: (jnp.minimum(linear_step(*g), n_chunks - 1), 0)
        self.w = w
        self.in_spec = pl.BlockSpec((chunk, cols), index)
        self.out_spec = pl.BlockSpec((chunk, cols), index)
        self.out_shape = jax.ShapeDtypeStruct((rows, cols), BF16)


def _with_sides(body, n_in, n_out, n_side):
    if n_side == 0:
        return body

    def wrapped(*refs):
        side_in = refs[n_in:n_in + n_side]
        o0 = n_in + n_side
        side_out = refs[o0 + n_out:o0 + n_out + n_side]
        for src, dst in zip(side_in, side_out):
            dst[...] = src[...].astype(dst.dtype)
        body(*refs[:n_in], *refs[o0:o0 + n_out], *refs[o0 + n_out + n_side:])

    return wrapped


def _call(body, grid, in_specs, out_specs, out_shapes, args, sides, semantics, name,
          scratch_shapes=()):
    n_in, n_out = len(in_specs), len(out_specs)
    outs = pl.pallas_call(
        _with_sides(body, n_in, n_out, len(sides)),
        grid=grid,
        in_specs=list(in_specs) + [s.in_spec for s in sides],
        out_specs=list(out_specs) + [s.out_spec for s in sides],
        out_shape=list(out_shapes) + [s.out_shape for s in sides],
        scratch_shapes=list(scratch_shapes),
        compiler_params=_params(*semantics),
        name=name,
    )(*args, *[s.w for s in sides])
    return outs[:n_out], outs[n_out:]


def _mm_kernel(*refs, has_res, d_scale, has_gain):
    refs = list(refs)
    a_ref, w_ref = refs.pop(0), refs.pop(0)
    r_ref = refs.pop(0) if has_res else None
    s_ref = refs.pop(0) if d_scale else None
    g_ref = refs.pop(0) if has_gain else None
    o_ref = refs.pop(0)
    acc = jnp.dot(a_ref[...], w_ref[...], preferred_element_type=F32)
    if d_scale:
        acc = acc * _inv_rms(s_ref, d_scale)
    if has_res:
        acc = r_ref[...] + acc
    o_ref[...] = acc.astype(o_ref.dtype)
    if has_gain:
        _emit_scaled(acc, g_ref, refs[0], refs[1], pl.program_id(1) == 0)


def matmul(a, w, out_dtype, tiles, *, name, residual=None, col_start=0, n=None,
           in_ssq=None, out_gain=None, converts=()):
    m, k = a.shape
    n = w.shape[1] if n is None else n
    tm, tn = _tile(m, tiles[0]), _tile(n, tiles[1])
    assert col_start % tn == 0
    j0 = col_start // tn
    gj = n // tn
    in_specs = [pl.BlockSpec((tm, k), lambda i, j: (i, 0)),
                pl.BlockSpec((k, tn), lambda i, j: (0, j0 + j))]
    args = [a, w]
    if residual is not None:
        in_specs.append(pl.BlockSpec((tm, tn), lambda i, j: (i, j)))
        args.append(residual)
    if in_ssq is not None:
        in_specs.append(pl.BlockSpec((tm, LANES), lambda i, j: (i, 0)))
        args.append(in_ssq)
    out_specs = [pl.BlockSpec((tm, tn), lambda i, j: (i, j))]
    out_shapes = [jax.ShapeDtypeStruct((m, n), out_dtype)]
    if out_gain is not None:
        in_specs.append(pl.BlockSpec((1, tn), lambda i, j: (0, j)))
        args.append(out_gain.reshape(1, n))
        out_specs += [pl.BlockSpec((tm, tn), lambda i, j: (i, j)),
                      pl.BlockSpec((tm, LANES), lambda i, j: (i, 0))]
        out_shapes += [jax.ShapeDtypeStruct((m, n), BF16),
                       jax.ShapeDtypeStruct((m, LANES), F32)]
    body = functools.partial(_mm_kernel, has_res=residual is not None,
                             d_scale=k if in_ssq is not None else 0,
                             has_gain=out_gain is not None)
    sides = [_Side(c, (m // tm) * gj, lambda i, j: i * gj + j) for c in converts]
    return _call(body, (m // tm, gj), in_specs, out_specs, out_shapes, args, sides,
                 ("parallel", "arbitrary"), name)


def _glu_kernel(a_ref, wg_ref, wu_ref, s_ref, o_ref, *, d_scale):
    a = a_ref[...]
    inv = _inv_rms(s_ref, d_scale)
    g = jnp.dot(a, wg_ref[...], preferred_element_type=F32) * inv
    u = jnp.dot(a, wu_ref[...], preferred_element_type=F32) * inv
    o_ref[...] = (g * jax.nn.sigmoid(g) * u).astype(o_ref.dtype)


def glu_matmul(a, wg, wu, in_ssq, converts=()):
    m, k = a.shape
    n = wg.shape[1]
    tm, tn = _tile(m, GLU_TILES[0]), _tile(n, GLU_TILES[1])
    gj = n // tn
    sides = [_Side(c, (m // tm) * gj, lambda i, j: i * gj + j) for c in converts]
    return _call(
        functools.partial(_glu_kernel, d_scale=k), (m // tm, gj),
        [pl.BlockSpec((tm, k), lambda i, j: (i, 0)),
         pl.BlockSpec((k, tn), lambda i, j: (0, j)),
         pl.BlockSpec((k, tn), lambda i, j: (0, j)),
         pl.BlockSpec((tm, LANES), lambda i, j: (i, 0))],
        [pl.BlockSpec((tm, tn), lambda i, j: (i, j))],
        [jax.ShapeDtypeStruct((m, n), BF16)],
        [a, wg, wu, in_ssq], sides, ("parallel", "arbitrary"), "ffn_gate_up")


def _ple_kernel(a_ref, w_ref, s_ref, h_ref, p_ref, wp_ref, gp_ref, gf_ref, o_ref, pe_ref, *,
                tn, d_scale, final_norm):
    j = pl.program_id(1)

    @pl.when(j == 0)
    def _():
        pe_ref[...] = _rms(jnp.dot(p_ref[...], wp_ref[...], preferred_element_type=F32),
                           gp_ref[...])

    cols = pl.ds(pl.multiple_of(j * tn, tn), tn)
    z = jnp.dot(a_ref[...], w_ref[...], preferred_element_type=F32) * _inv_rms(s_ref, d_scale)
    o_ref[:, cols] = h_ref[...] + jax.nn.sigmoid(z) * pe_ref[:, cols]

    if final_norm:
        @pl.when(j == pl.num_programs(1) - 1)
        def _():
            o_ref[...] = _rms(o_ref[...], gf_ref[...])


def gated_embedding_add(a, w, in_ssq, h, p, wp, gp, gf, final_norm):
    m, k = a.shape
    n = w.shape[1]
    kp = p.shape[1]
    tm, tn = _tile(m, PLE_TILES[0]), _tile(n, PLE_TILES[1])
    kern = functools.partial(_ple_kernel, tn=tn, d_scale=k, final_norm=final_norm)
    return pl.pallas_call(
        kern,
        grid=(m // tm, n // tn),
        in_specs=[pl.BlockSpec((tm, k), lambda i, j: (i, 0)),
                  pl.BlockSpec((k, tn), lambda i, j: (0, j)),
                  pl.BlockSpec((tm, LANES), lambda i, j: (i, 0)),
                  pl.BlockSpec((tm, tn), lambda i, j: (i, j)),
                  pl.BlockSpec((tm, kp), lambda i, j: (i, 0)),
                  pl.BlockSpec((kp, n), lambda i, j: (0, 0)),
                  pl.BlockSpec((1, n), lambda i, j: (0, 0)),
                  pl.BlockSpec((1, n), lambda i, j: (0, 0))],
        out_specs=pl.BlockSpec((tm, n), lambda i, j: (i, 0)),
        out_shape=jax.ShapeDtypeStruct((m, n), F32),
        scratch_shapes=[pltpu.VMEM((tm, n), F32)],
        compiler_params=_params("parallel", "arbitrary"),
        name="ple_gate",
    )(a, w, in_ssq, h, p, wp, gp.reshape(1, n), gf.reshape(1, n))


def _suffix_sum_matrix(n):
    r = lax.broadcasted_iota(jnp.int32, (2 * n, n), 0) % n
    c = lax.broadcasted_iota(jnp.int32, (2 * n, n), 1)
    return jnp.where(r >= c, 1.0, 0.0).astype(BF16)


def _attn_kernel(q_ref, k_ref, v_ref, t_ref, o_ref, acc_ref, carry_ref, *, tq, heads, scale):
    qi = pl.program_id(2)
    groups = tq // LANES
    sign_bit = jnp.uint32(0x80000000)

    def all_heads(kstart, masked):
        hs = range(heads)
        cols = [slice(h * D_HEAD, (h + 1) * D_HEAD) for h in hs]
        if masked:
            row = lax.broadcasted_iota(jnp.int32, (tq, tq), 0)
            col = lax.broadcasted_iota(jnp.int32, (tq, tq), 1)
            causal = col < row
        nz = [lax.dot_general(q_ref[:, cols[h]], k_ref[pl.ds(kstart, tq), cols[h]],
                              (((1,), (1,)), ((), ())), preferred_element_type=F32)
              * (-scale * LOG2_E) for h in hs]
        sums = []
        for h in hs:
            neg_abs = lax.bitcast_convert_type(
                lax.bitcast_convert_type(nz[h], jnp.uint32) | sign_bit, F32)
            log_keep = jnp.minimum(nz[h], 0.0) - jnp.log(1.0 + jnp.exp2(neg_abs)) * LOG2_E
            if masked:
                log_keep = jnp.where(causal, log_keep, 0.0)
            hi = log_keep.astype(BF16)
            lo = (log_keep - hi.astype(F32)).astype(BF16)
            sums.append(jnp.dot(jnp.concatenate([hi, lo], axis=1), t_ref[...],
                                preferred_element_type=F32))
        top = None
        for h in hs:
            carry = carry_ref[h]
            w = jnp.exp2(sums[h] - nz[h] + jnp.concatenate([carry] * groups, axis=1))
            if masked:
                w = jnp.where(causal, w, 0.0)
            acc_ref[h] += jnp.dot(w.astype(BF16), v_ref[pl.ds(kstart, tq), cols[h]],
                                  preferred_element_type=F32)
            carry = carry + jnp.broadcast_to(sums[h][:, 0:1], (tq, LANES))
            carry_ref[h] = carry
            m = jnp.max(carry)
            top = m if top is None else jnp.maximum(top, m)
        return top

    for h in range(heads):
        carry_ref[h] = jnp.zeros((tq, LANES), F32)
        acc_ref[h] = jnp.zeros((tq, D_HEAD), F32)
    top = all_heads(pl.multiple_of(qi * tq, tq), True)

    def more(c):
        j, top = c
        return jnp.logical_and(j < qi, top >= MIN_LOG2_WEIGHT)

    def body(c):
        j, _ = c
        return j + 1, all_heads(pl.multiple_of((qi - 1 - j) * tq, tq), False)

    lax.while_loop(more, body, (jnp.int32(0), top))

    for h in range(heads):
        o_ref[:, h * D_HEAD:(h + 1) * D_HEAD] = acc_ref[h]


def stick_breaking_attention(qkv, batch, seq, d_attn, converts=()):
    n_heads = d_attn // D_HEAD
    heads = min(ATTN_HEADS, n_heads)
    assert n_heads % heads == 0
    width = heads * D_HEAD
    pairs = d_attn // width
    tq = _tile(seq, ATTN_ROWS)
    nq = seq // tq
    kern = functools.partial(_attn_kernel, tq=tq, heads=heads, scale=D_HEAD ** -0.5)
    sides = [_Side(c, batch * pairs * nq, lambda b, p, i: (b * pairs + p) * nq + i)
             for c in converts]
    return _call(
        kern, (batch, pairs, nq),
        [pl.BlockSpec((tq, width), lambda b, p, i: (b * nq + i, p)),
         pl.BlockSpec((seq, width), lambda b, p, i: (b, pairs + p)),
         pl.BlockSpec((seq, width), lambda b, p, i: (b, 2 * pairs + p)),
         pl.BlockSpec((2 * tq, tq), lambda b, p, i: (0, 0))],
        [pl.BlockSpec((tq, width), lambda b, p, i: (b * nq + i, p))],
        [jax.ShapeDtypeStruct((batch * seq, d_attn), F32)],
        [qkv, qkv, qkv, _suffix_sum_matrix(tq)], sides,
        ("parallel", "parallel", "arbitrary"), "stick_breaking_attention",
        scratch_shapes=[pltpu.VMEM((heads, tq, D_HEAD), F32),
                        pltpu.VMEM((heads, tq, LANES), F32)])


def _softplus(x):
    return jnp.maximum(x, 0.0) + jnp.log1p(jnp.exp(-jnp.abs(x)))


def _mixer_kernel(xr_ref, gr_ref, attn_ref, cw_ref, cb_ref, wg_ref, ba_ref, bx_ref, lam_ref,
                  ga_ref, gn_ref, o_ref, tail_ref, h_ref, rec_ref, *, ts, n_blocks, conv_width):
    @pl.when(pl.program_id(1) == 0)
    def _():
        tail_ref[...] = jnp.zeros_like(tail_ref)
        h_ref[...] = jnp.zeros_like(h_ref)

    row = lax.broadcasted_iota(jnp.int32, (ts, LANES), 0)
    row8 = lax.broadcasted_iota(jnp.int32, (SUBLANES, LANES), 0)
    sub = row % SUBLANES

    def block(n, ssq):
        lanes = pl.ds(pl.multiple_of(n * LANES, LANES), LANES)
        x = xr_ref[:, lanes]
        tail = tail_ref[:, lanes]
        y = cb_ref[:, lanes] + cw_ref[conv_width - 1:conv_width, lanes] * x
        for d in range(1, conv_width):
            xs = pltpu.roll(x, d, axis=0)
            head = jnp.where(row8 < d, pltpu.roll(tail, d, axis=0), xs[:SUBLANES])
            xs = jnp.concatenate([head, xs[SUBLANES:]], axis=0)
            y = y + cw_ref[conv_width - 1 - d:conv_width - d, lanes] * xs
        tail_ref[:, lanes] = x[ts - SUBLANES:]

        gates = jnp.dot(y.astype(BF16), wg_ref[n], preferred_element_type=F32)
        r = jax.nn.sigmoid(gates[:, :LANES] + ba_ref[:, lanes])
        i = jax.nn.sigmoid(gates[:, LANES:] + bx_ref[:, lanes])
        log_a = -RGLRU_C * r * _softplus(-lam_ref[:, lanes])
        a = jnp.exp(log_a)
        v = -jnp.tanh(log_a) * (a * a + 1.0)
        u = jnp.where(v > 0.0, v * lax.rsqrt(v), 0.0) * (i * y)

        for d in (1, 2, 4):
            keep = sub >= d
            a_prev = jnp.where(keep, pltpu.roll(a, d, axis=0), 1.0)
            u_prev = jnp.where(keep, pltpu.roll(u, d, axis=0), 0.0)
            u = a * u_prev + u
            a = a * a_prev
        h_prev = h_ref[:, lanes]
        hs = []
        for g in range(ts // SUBLANES):
            rows = slice(g * SUBLANES, (g + 1) * SUBLANES)
            hg = a[rows] * h_prev + u[rows]
            hs.append(hg)
            h_prev = jnp.broadcast_to(hg[SUBLANES - 1:SUBLANES], (SUBLANES, LANES))
        h_ref[:, lanes] = h_prev
        rec = jnp.concatenate(hs, axis=0) * jax.nn.gelu(gr_ref[:, lanes], approximate=True)
        rec_ref[:, lanes] = rec
        return ssq + jnp.sum(rec * rec, axis=1, keepdims=True)

    ssq = lax.fori_loop(0, n_blocks, block, jnp.zeros((ts, 1), F32))
    d_attn = attn_ref.shape[1]
    d_rnn = n_blocks * LANES
    o_ref[:, :d_attn] = _rms(attn_ref[...], ga_ref[...]).astype(o_ref.dtype)
    inv = lax.rsqrt(ssq / d_rnn + NORM_EPS)
    o_ref[:, d_attn:] = (rec_ref[...] * inv * gn_ref[...]).astype(o_ref.dtype)


def mixer_epilogue(xg, attn, conv_w, conv_b, w_gates, b_a, b_x, lam, g_attn, g_rnn, batch, seq):
    d_rnn = xg.shape[1] // 2
    d_attn = attn.shape[1]
    n_blocks = d_rnn // LANES
    assert w_gates.shape == (n_blocks, LANES, 2 * LANES)
    conv_width = conv_w.shape[0]
    ts = _tile(seq, SCAN_ROWS)
    ns = seq // ts
    row1 = lambda v: v.reshape(1, -1)
    const = lambda shape: pl.BlockSpec(shape, lambda b, s: (0,) * len(shape))
    kern = functools.partial(_mixer_kernel, ts=ts, n_blocks=n_blocks, conv_width=conv_width)
    return pl.pallas_call(
        kern,
        grid=(batch, ns),
        in_specs=[pl.BlockSpec((ts, d_rnn), lambda b, s: (b * ns + s, 0)),
                  pl.BlockSpec((ts, d_rnn), lambda b, s: (b * ns + s, 1)),
                  pl.BlockSpec((ts, d_attn), lambda b, s: (b * ns + s, 0)),
                  const((conv_width, d_rnn)), const((1, d_rnn)),
                  const((n_blocks, LANES, 2 * LANES)),
                  const((1, d_rnn)), const((1, d_rnn)), const((1, d_rnn)),
                  const((1, d_attn)), const((1, d_rnn))],
        out_specs=pl.BlockSpec((ts, d_attn + d_rnn), lambda b, s: (b * ns + s, 0)),
        out_shape=jax.ShapeDtypeStruct((batch * seq, d_attn + d_rnn), BF16),
        scratch_shapes=[pltpu.VMEM((SUBLANES, d_rnn), F32),
                        pltpu.VMEM((SUBLANES, d_rnn), F32),
                        pltpu.VMEM((ts, d_rnn), F32)],
        compiler_params=_params("parallel", "arbitrary"),
        name="rglru_mixer",
    )(xg, xg, attn, conv_w, row1(conv_b), w_gates, row1(b_a), row1(b_x), row1(lam),
      row1(g_attn), row1(g_rnn))


def kernel(x, p, g_mix, w_in, conv_w, conv_b, w_rg_a, b_rg_a, w_rg_x, b_rg_x, rg_lambda,
           g_attn_out, g_rnn_out, w_out, g_ffn, w_ffn_gate, w_ffn_up, w_ffn_down, g_ple,
           w_ple_gate, w_ple_proj, g_ple_out, g_final):
    batch, seq, d_model = x.shape
    depth = w_in.shape[0]
    d_attn = g_attn_out.shape[-1]
    d_rnn = g_rnn_out.shape[-1]
    m = batch * seq

    h = x.reshape(m, d_model)
    for l in range(depth):
        u = rmsnorm(h, g_mix[l], BF16)
        w_in_l = w_in[l].astype(BF16)
        (qkv,), (w_out_l,) = matmul(u, w_in_l, BF16, PROJ_TILES, name="in_proj_qkv",
                                    n=3 * d_attn, converts=[w_out[l]])
        (xg,), (w_ple_l,) = matmul(u, w_in_l, F32, PROJ_TILES, name="in_proj_rnn",
                                   col_start=3 * d_attn, n=2 * d_rnn, converts=[w_ple_gate[l]])
        (attn,), (w_gate_l, w_up_l) = stick_breaking_attention(
            qkv, batch, seq, d_attn, converts=[w_ffn_gate[l], w_ffn_up[l]])
        w_gates = jnp.concatenate([w_rg_a[l], w_rg_x[l]], axis=-1).astype(BF16)
        mixed = mixer_epilogue(xg, attn, conv_w[l], conv_b[l], w_gates, b_rg_a[l], b_rg_x[l],
                               rg_lambda[l], g_attn_out[l], g_rnn_out[l], batch, seq)
        (h, hg, ssq), _ = matmul(mixed, w_out_l, F32, OUT_TILES, residual=h,
                                 out_gain=g_ffn[l], name="out_proj")

        (hidden,), (w_down_l,) = glu_matmul(hg, w_gate_l, w_up_l, ssq, converts=[w_ffn_down[l]])
        (h, hg, ssq), _ = matmul(hidden, w_down_l, F32, DOWN_TILES, residual=h,
                                 out_gain=g_ple[l], name="ffn_down")

        h = gated_embedding_add(hg, w_ple_l, ssq, h,
                                p[l].reshape(m, -1).astype(BF16), w_ple_proj[l].astype(BF16),
                                g_ple_out[l], g_final, final_norm=(l == depth - 1))
    return h.reshape(batch, seq, d_model)
```

```python
import functools

import jax
import jax.numpy as jnp
from jax import lax
from jax.experimental import pallas as pl
from jax.experimental.pallas import tpu as pltpu

F32 = jnp.float32
BF16 = jnp.bfloat16

NORM_EPS = 1e-6
D_HEAD = 128
RGLRU_C = 8.0
LOG2_E = 1.4426950408889634
MIN_LOG2_WEIGHT = -150.0

LANES = 128
SUBLANES = 8
VMEM_LIMIT_BYTES = 60 * 1024 * 1024

PROJ_TILES = (1024, 1024)
OUT_TILES = (1024, 1024)
GLU_TILES = (2048, 256)
DOWN_TILES = (512, 512)
PLE_TILES = (512, 512)
NORM_ROWS = 256
ATTN_ROWS = 256
ATTN_HEADS = 8
SCAN_ROWS = 256
EPILOGUE_ROWS = 256


def _params(*semantics):
    return pltpu.CompilerParams(dimension_semantics=semantics,
                                vmem_limit_bytes=VMEM_LIMIT_BYTES)


def _tile(n, want):
    t = min(n, want)
    assert n % t == 0, (n, want)
    return t


def _rms(x, g):
    y = x * lax.rsqrt(jnp.mean(x * x, axis=-1, keepdims=True) + NORM_EPS)
    return y * g


def _rmsnorm_kernel(x_ref, g_ref, o_ref):
    o_ref[...] = _rms(x_ref[...], g_ref[...]).astype(o_ref.dtype)


def rmsnorm(x, g, out_dtype):
    m, d = x.shape
    tm = _tile(m, NORM_ROWS)
    return pl.pallas_call(
        _rmsnorm_kernel,
        grid=(m // tm,),
        in_specs=[pl.BlockSpec((tm, d), lambda i: (i, 0)),
                  pl.BlockSpec((1, d), lambda i: (0, 0))],
        out_specs=pl.BlockSpec((tm, d), lambda i: (i, 0)),
        out_shape=jax.ShapeDtypeStruct((m, d), out_dtype),
        compiler_params=_params("parallel"),
        name="rmsnorm",
    )(x, g.reshape(1, d))


def _row_chunks(rows):
    ch = min(rows, EPILOGUE_ROWS)
    assert rows % ch == 0
    return [pl.ds(c * ch, ch) for c in range(rows // ch)]


def _inv_rms(ssq_ref, rows, d):
    return lax.rsqrt(ssq_ref[rows, 0:1] * (1.0 / d) + NORM_EPS)


def _emit_scaled(h, rows, g_ref, hg_ref, ssq_ref):
    hg_ref[rows, :] = (h * g_ref[...]).astype(hg_ref.dtype)
    part = jnp.sum(h * h, axis=1, keepdims=True)
    ssq_ref[rows, :] += jnp.broadcast_to(part, (h.shape[0], ssq_ref.shape[1]))


class _Side:
    def __init__(self, w, n_steps, linear_step):
        rows, cols = w.shape
        units = rows // (2 * SUBLANES)
        assert units * 2 * SUBLANES == rows
        n_chunks = max(c for c in range(1, min(units, n_steps) + 1) if units % c == 0)
        chunk = rows // n_chunks
        index = lambda *g: (jnp.minimum(linear_step(*g), n_chunks - 1), 0)
        self.w = w
        self.in_spec = pl.BlockSpec((chunk, cols), index)
        self.out_spec = pl.BlockSpec((chunk, cols), index)
        self.out_shape = jax.ShapeDtypeStruct((rows, cols), BF16)


def _with_sides(body, n_in, n_out, n_side):
    if n_side == 0:
        return body

    def wrapped(*refs):
        side_in = refs[n_in:n_in + n_side]
        o0 = n_in + n_side
        side_out = refs[o0 + n_out:o0 + n_out + n_side]
        for src, dst in zip(side_in, side_out):
            dst[...] = src[...].astype(dst.dtype)
        body(*refs[:n_in], *refs[o0:o0 + n_out], *refs[o0 + n_out + n_side:])

    return wrapped


def _call(body, grid, in_specs, out_specs, out_shapes, args, sides, semantics, name,
          scratch_shapes=()):
    n_in, n_out = len(in_specs), len(out_specs)
    outs = pl.pallas_call(
        _with_sides(body, n_in, n_out, len(sides)),
        grid=grid,
        in_specs=list(in_specs) + [s.in_spec for s in sides],
        out_specs=list(out_specs) + [s.out_spec for s in sides],
        out_shape=list(out_shapes) + [s.out_shape for s in sides],
        scratch_shapes=list(scratch_shapes),
        compiler_params=_params(*semantics),
        name=name,
    )(*args, *[s.w for s in sides])
    return outs[:n_out], outs[n_out:]


def _mm_kernel(*refs, has_res, d_scale, has_gain):
    refs = list(refs)
    a_ref, w_ref = refs.pop(0), refs.pop(0)
    r_ref = refs.pop(0) if has_res else None
    s_ref = refs.pop(0) if d_scale else None
    g_ref = refs.pop(0) if has_gain else None
    o_ref = refs.pop(0)
    if has_gain:
        hg_ref, ssq_ref = refs

        @pl.when(pl.program_id(1) == 0)
        def _():
            ssq_ref[...] = jnp.zeros_like(ssq_ref)

    for rows in _row_chunks(a_ref.shape[0]):
        acc = jnp.dot(a_ref[rows, :], w_ref[...], preferred_element_type=F32)
        if d_scale:
            acc = acc * _inv_rms(s_ref, rows, d_scale)
        if has_res:
            acc = r_ref[rows, :] + acc
        o_ref[rows, :] = acc.astype(o_ref.dtype)
        if has_gain:
            _emit_scaled(acc, rows, g_ref, hg_ref, ssq_ref)


def matmul(a, w, out_dtype, tiles, *, name, residual=None, col_start=0, n=None,
           in_ssq=None, out_gain=None, converts=()):
    m, k = a.shape
    n = w.shape[1] if n is None else n
    tm, tn = _tile(m, tiles[0]), _tile(n, tiles[1])
    assert col_start % tn == 0
    j0 = col_start // tn
    gj = n // tn
    in_specs = [pl.BlockSpec((tm, k), lambda i, j: (i, 0)),
                pl.BlockSpec((k, tn), lambda i, j: (0, j0 + j))]
    args = [a, w]
    if residual is not None:
        in_specs.append(pl.BlockSpec((tm, tn), lambda i, j: (i, j)))
        args.append(residual)
    if in_ssq is not None:
        in_specs.append(pl.BlockSpec((tm, LANES), lambda i, j: (i, 0)))
        args.append(in_ssq)
    out_specs = [pl.BlockSpec((tm, tn), lambda i, j: (i, j))]
    out_shapes = [jax.ShapeDtypeStruct((m, n), out_dtype)]
    if out_gain is not None:
        in_specs.append(pl.BlockSpec((1, tn), lambda i, j: (0, j)))
        args.append(out_gain.reshape(1, n))
        out_specs += [pl.BlockSpec((tm, tn), lambda i, j: (i, j)),
                      pl.BlockSpec((tm, LANES), lambda i, j: (i, 0))]
        out_shapes += [jax.ShapeDtypeStruct((m, n), BF16),
                       jax.ShapeDtypeStruct((m, LANES), F32)]
    body = functools.partial(_mm_kernel, has_res=residual is not None,
                             d_scale=k if in_ssq is not None else 0,
                             has_gain=out_gain is not None)
    sides = [_Side(c, (m // tm) * gj, lambda i, j: i * gj + j) for c in converts]
    return _call(body, (m // tm, gj), in_specs, out_specs, out_shapes, args, sides,
                 ("parallel", "arbitrary"), name)


def _glu_kernel(a_ref, wg_ref, wu_ref, s_ref, o_ref, *, d_scale):
    for rows in _row_chunks(a_ref.shape[0]):
        a = a_ref[rows, :]
        inv = _inv_rms(s_ref, rows, d_scale)
        g = jnp.dot(a, wg_ref[...], preferred_element_type=F32) * inv
        u = jnp.dot(a, wu_ref[...], preferred_element_type=F32) * inv
        o_ref[rows, :] = (g * jax.nn.sigmoid(g) * u).astype(o_ref.dtype)


def glu_matmul(a, wg, wu, in_ssq, converts=()):
    m, k = a.shape
    n = wg.shape[1]
    tm, tn = _tile(m, GLU_TILES[0]), _tile(n, GLU_TILES[1])
    gj = n // tn
    sides = [_Side(c, (m // tm) * gj, lambda i, j: i * gj + j) for c in converts]
    return _call(
        functools.partial(_glu_kernel, d_scale=k), (m // tm, gj),
        [pl.BlockSpec((tm, k), lambda i, j: (i, 0)),
         pl.BlockSpec((k, tn), lambda i, j: (0, j)),
         pl.BlockSpec((k, tn), lambda i, j: (0, j)),
         pl.BlockSpec((tm, LANES), lambda i, j: (i, 0))],
        [pl.BlockSpec((tm, tn), lambda i, j: (i, j))],
        [jax.ShapeDtypeStruct((m, n), BF16)],
        [a, wg, wu, in_ssq], sides, ("parallel", "arbitrary"), "ffn_gate_up")


def _ple_kernel(a_ref, w_ref, s_ref, h_ref, p_ref, wp_ref, gp_ref, gf_ref, o_ref, pe_ref, *,
                tn, d_scale, final_norm):
    j = pl.program_id(1)

    @pl.when(j == 0)
    def _():
        pe_ref[...] = _rms(jnp.dot(p_ref[...], wp_ref[...], preferred_element_type=F32),
                           gp_ref[...])

    cols = pl.ds(pl.multiple_of(j * tn, tn), tn)
    for rows in _row_chunks(a_ref.shape[0]):
        z = (jnp.dot(a_ref[rows, :], w_ref[...], preferred_element_type=F32)
             * _inv_rms(s_ref, rows, d_scale))
        o_ref[rows, cols] = h_ref[rows, :] + jax.nn.sigmoid(z) * pe_ref[rows, cols]

    if final_norm:
        @pl.when(j == pl.num_programs(1) - 1)
        def _():
            o_ref[...] = _rms(o_ref[...], gf_ref[...])


def gated_embedding_add(a, w, in_ssq, h, p, wp, gp, gf, final_norm):
    m, k = a.shape
    n = w.shape[1]
    kp = p.shape[1]
    tm, tn = _tile(m, PLE_TILES[0]), _tile(n, PLE_TILES[1])
    kern = functools.partial(_ple_kernel, tn=tn, d_scale=k, final_norm=final_norm)
    return pl.pallas_call(
        kern,
        grid=(m // tm, n // tn),
        in_specs=[pl.BlockSpec((tm, k), lambda i, j: (i, 0)),
                  pl.BlockSpec((k, tn), lambda i, j: (0, j)),
                  pl.BlockSpec((tm, LANES), lambda i, j: (i, 0)),
                  pl.BlockSpec((tm, tn), lambda i, j: (i, j)),
                  pl.BlockSpec((tm, kp), lambda i, j: (i, 0)),
                  pl.BlockSpec((kp, n), lambda i, j: (0, 0)),
                  pl.BlockSpec((1, n), lambda i, j: (0, 0)),
                  pl.BlockSpec((1, n), lambda i, j: (0, 0))],
        out_specs=pl.BlockSpec((tm, n), lambda i, j: (i, 0)),
        out_shape=jax.ShapeDtypeStruct((m, n), F32),
        scratch_shapes=[pltpu.VMEM((tm, n), F32)],
        compiler_params=_params("parallel", "arbitrary"),
        name="ple_gate",
    )(a, w, in_ssq, h, p, wp, gp.reshape(1, n), gf.reshape(1, n))


def _suffix_sum_matrix(n):
    r = lax.broadcasted_iota(jnp.int32, (2 * n, n), 0) % n
    c = lax.broadcasted_iota(jnp.int32, (2 * n, n), 1)
    return jnp.where(r >= c, 1.0, 0.0).astype(BF16)


def _attn_kernel(q_ref, k_ref, v_ref, t_ref, o_ref, acc_ref, carry_ref, *, tq, heads, scale):
    qi = pl.program_id(2)
    groups = tq // LANES
    sign_bit = jnp.uint32(0x80000000)

    def all_heads(kstart, masked):
        hs = range(heads)
        cols = [slice(h * D_HEAD, (h + 1) * D_HEAD) for h in hs]
        if masked:
            row = lax.broadcasted_iota(jnp.int32, (tq, tq), 0)
            col = lax.broadcasted_iota(jnp.int32, (tq, tq), 1)
            causal = col < row
        nz = [lax.dot_general(q_ref[:, cols[h]], k_ref[pl.ds(kstart, tq), cols[h]],
                              (((1,), (1,)), ((), ())), preferred_element_type=F32)
              * (-scale * LOG2_E) for h in hs]
        sums = []
        for h in hs:
            neg_abs = lax.bitcast_convert_type(
                lax.bitcast_convert_type(nz[h], jnp.uint32) | sign_bit, F32)
            log_keep = jnp.minimum(nz[h], 0.0) - jnp.log(1.0 + jnp.exp2(neg_abs)) * LOG2_E
            if masked:
                log_keep = jnp.where(causal, log_keep, 0.0)
            hi = log_keep.astype(BF16)
            lo = (log_keep - hi.astype(F32)).astype(BF16)
            sums.append(jnp.dot(jnp.concatenate([hi, lo], axis=1), t_ref[...],
                                preferred_element_type=F32))
        top = None
        for h in hs:
            carry = carry_ref[h]
            w = jnp.exp2(sums[h] - nz[h] + jnp.concatenate([carry] * groups, axis=1))
            if masked:
                w = jnp.where(causal, w, 0.0)
            acc_ref[h] += jnp.dot(w.astype(BF16), v_ref[pl.ds(kstart, tq), cols[h]],
                                  preferred_element_type=F32)
            carry = carry + jnp.broadcast_to(sums[h][:, 0:1], (tq, LANES))
            carry_ref[h] = carry
            m = jnp.max(carry)
            top = m if top is None else jnp.maximum(top, m)
        return top

    for h in range(heads):
        carry_ref[h] = jnp.zeros((tq, LANES), F32)
        acc_ref[h] = jnp.zeros((tq, D_HEAD), F32)
    top = all_heads(pl.multiple_of(qi * tq, tq), True)

    def more(c):
        j, top = c
        return jnp.logical_and(j < qi, top >= MIN_LOG2_WEIGHT)

    def body(c):
        j, _ = c
        return j + 1, all_heads(pl.multiple_of((qi - 1 - j) * tq, tq), False)

    lax.while_loop(more, body, (jnp.int32(0), top))

    for h in range(heads):
        o_ref[:, h * D_HEAD:(h + 1) * D_HEAD] = acc_ref[h]


def stick_breaking_attention(qkv, batch, seq, d_attn, converts=()):
    n_heads = d_attn // D_HEAD
    heads = min(ATTN_HEADS, n_heads)
    assert n_heads % heads == 0
    width = heads * D_HEAD
    pairs = d_attn // width
    tq = _tile(seq, ATTN_ROWS)
    nq = seq // tq
    kern = functools.partial(_attn_kernel, tq=tq, heads=heads, scale=D_HEAD ** -0.5)
    sides = [_Side(c, batch * pairs * nq, lambda b, p, i: (b * pairs + p) * nq + i)
             for c in converts]
    return _call(
        kern, (batch, pairs, nq),
        [pl.BlockSpec((tq, width), lambda b, p, i: (b * nq + i, p)),
         pl.BlockSpec((seq, width), lambda b, p, i: (b, pairs + p)),
         pl.BlockSpec((seq, width), lambda b, p, i: (b, 2 * pairs + p)),
         pl.BlockSpec((2 * tq, tq), lambda b, p, i: (0, 0))],
        [pl.BlockSpec((tq, width), lambda b, p, i: (b * nq + i, p))],
        [jax.ShapeDtypeStruct((batch * seq, d_attn), F32)],
        [qkv, qkv, qkv, _suffix_sum_matrix(tq)], sides,
        ("parallel", "parallel", "arbitrary"), "stick_breaking_attention",
        scratch_shapes=[pltpu.VMEM((heads, tq, D_HEAD), F32),
                        pltpu.VMEM((heads, tq, LANES), F32)])


def _softplus(x):
    return jnp.maximum(x, 0.0) + jnp.log1p(jnp.exp(-jnp.abs(x)))


def _mixer_kernel(xr_ref, gr_ref, attn_ref, cw_ref, cb_ref, wg_ref, ba_ref, bx_ref, lam_ref,
                  ga_ref, gn_ref, o_ref, tail_ref, h_ref, rec_ref, *, ts, n_blocks, conv_width):
    @pl.when(pl.program_id(1) == 0)
    def _():
        tail_ref[...] = jnp.zeros_like(tail_ref)
        h_ref[...] = jnp.zeros_like(h_ref)

    row8 = lax.broadcasted_iota(jnp.int32, (SUBLANES, LANES), 0)
    sub3 = lax.broadcasted_iota(jnp.int32, (ts // SUBLANES, SUBLANES, LANES), 1)

    def block(n, ssq):
        lanes = pl.ds(pl.multiple_of(n * LANES, LANES), LANES)
        x = xr_ref[:, lanes]
        tail = tail_ref[:, lanes]
        assert conv_width - 1 < SUBLANES
        y = cb_ref[:, lanes] + cw_ref[conv_width - 1:conv_width, lanes] * x
        x3 = x.reshape(ts // SUBLANES, SUBLANES, LANES)
        for d in range(1, conv_width):
            rot = pltpu.roll(x3, d, axis=1)
            prev = jnp.concatenate([pltpu.roll(tail, d, axis=0)[None], rot[:-1]], axis=0)
            xs = jnp.where(sub3 >= d, rot, prev).reshape(ts, LANES)
            y = y + cw_ref[conv_width - 1 - d:conv_width - d, lanes] * xs
        tail_ref[:, lanes] = x[ts - SUBLANES:]

        gates = jnp.dot(y.astype(BF16), wg_ref[n], preferred_element_type=F32)
        r = jax.nn.sigmoid(gates[:, :LANES] + ba_ref[:, lanes])
        i = jax.nn.sigmoid(gates[:, LANES:] + bx_ref[:, lanes])
        log_a = -RGLRU_C * r * _softplus(-lam_ref[:, lanes])
        a = jnp.exp(log_a)
        v = -jnp.tanh(log_a) * (a * a + 1.0)
        u = jnp.where(v > 0.0, v * lax.rsqrt(v), 0.0) * (i * y)

        a = a.reshape(ts // SUBLANES, SUBLANES, LANES)
        u = u.reshape(ts // SUBLANES, SUBLANES, LANES)
        for d in (1, 2, 4):
            keep = sub3 >= d
            a_prev = jnp.where(keep, pltpu.roll(a, d, axis=1), 1.0)
            u_prev = jnp.where(keep, pltpu.roll(u, d, axis=1), 0.0)
            u = a * u_prev + u
            a = a * a_prev
        a = a.reshape(ts, LANES)
        u = u.reshape(ts, LANES)
        h_prev = h_ref[:, lanes]
        hs = []
        for g in range(ts // SUBLANES):
            rows = slice(g * SUBLANES, (g + 1) * SUBLANES)
            hg = a[rows] * h_prev + u[rows]
            hs.append(hg)
            h_prev = jnp.broadcast_to(hg[SUBLANES - 1:SUBLANES], (SUBLANES, LANES))
        h_ref[:, lanes] = h_prev
        rec = jnp.concatenate(hs, axis=0) * jax.nn.gelu(gr_ref[:, lanes], approximate=True)
        rec_ref[:, lanes] = rec
        return ssq + jnp.sum(rec * rec, axis=1, keepdims=True)

    ssq = lax.fori_loop(0, n_blocks, block, jnp.zeros((ts, 1), F32),
                        unroll=2 if n_blocks % 2 == 0 else 1)
    d_attn = attn_ref.shape[1]
    d_rnn = n_blocks * LANES
    inv = lax.rsqrt(ssq / d_rnn + NORM_EPS)
    step = 2 * SUBLANES
    for r0 in range(0, ts, step):
        rows = pl.ds(r0, step)
        o_ref[rows, :d_attn] = _rms(attn_ref[rows, :], ga_ref[...]).astype(o_ref.dtype)
        o_ref[rows, d_attn:] = (rec_ref[rows, :] * inv[r0:r0 + step]
                                * gn_ref[...]).astype(o_ref.dtype)


def mixer_epilogue(xg, attn, conv_w, conv_b, w_gates, b_a, b_x, lam, g_attn, g_rnn, batch, seq):
    d_rnn = xg.shape[1] // 2
    d_attn = attn.shape[1]
    n_blocks = d_rnn // LANES
    assert w_gates.shape == (n_blocks, LANES, 2 * LANES)
    conv_width = conv_w.shape[0]
    ts = _tile(seq, SCAN_ROWS)
    ns = seq // ts
    row1 = lambda v: v.reshape(1, -1)
    const = lambda shape: pl.BlockSpec(shape, lambda b, s: (0,) * len(shape))
    kern = functools.partial(_mixer_kernel, ts=ts, n_blocks=n_blocks, conv_width=conv_width)
    return pl.pallas_call(
        kern,
        grid=(batch, ns),
        in_specs=[pl.BlockSpec((ts, d_rnn), lambda b, s: (b * ns + s, 0)),
                  pl.BlockSpec((ts, d_rnn), lambda b, s: (b * ns + s, 1)),
                  pl.BlockSpec((ts, d_attn), lambda b, s: (b * ns + s, 0)),
                  const((conv_width, d_rnn)), const((1, d_rnn)),
                  const((n_blocks, LANES, 2 * LANES)),
                  const((1, d_rnn)), const((1, d_rnn)), const((1, d_rnn)),
                  const((1, d_attn)), const((1, d_rnn))],
        out_specs=pl.BlockSpec((ts, d_attn + d_rnn), lambda b, s: (b * ns + s, 0)),
        out_shape=jax.ShapeDtypeStruct((batch * seq, d_attn + d_rnn), BF16),
        scratch_shapes=[pltpu.VMEM((SUBLANES, d_rnn), F32),
                        pltpu.VMEM((SUBLANES, d_rnn), F32),
                        pltpu.VMEM((ts, d_rnn), F32)],
        compiler_params=_params("parallel", "arbitrary"),
        name="rglru_mixer",
    )(xg, xg, attn, conv_w, row1(conv_b), w_gates, row1(b_a), row1(b_x), row1(lam),
      row1(g_attn), row1(g_rnn))


def kernel(x, p, g_mix, w_in, conv_w, conv_b, w_rg_a, b_rg_a, w_rg_x, b_rg_x, rg_lambda,
           g_attn_out, g_rnn_out, w_out, g_ffn, w_ffn_gate, w_ffn_up, w_ffn_down, g_ple,
           w_ple_gate, w_ple_proj, g_ple_out, g_final):
    batch, seq, d_model = x.shape
    depth = w_in.shape[0]
    d_attn = g_attn_out.shape[-1]
    d_rnn = g_rnn_out.shape[-1]
    m = batch * seq

    h = x.reshape(m, d_model)
    for l in range(depth):
        u = rmsnorm(h, g_mix[l], BF16)
        w_in_l = w_in[l].astype(BF16)
        (qkv,), (w_out_l,) = matmul(u, w_in_l, BF16, PROJ_TILES, name="in_proj_qkv",
                                    n=3 * d_attn, converts=[w_out[l]])
        (xg,), (w_ple_l,) = matmul(u, w_in_l, F32, PROJ_TILES, name="in_proj_rnn",
                                   col_start=3 * d_attn, n=2 * d_rnn, converts=[w_ple_gate[l]])
        (attn,), (w_gate_l, w_up_l) = stick_breaking_attention(
            qkv, batch, seq, d_attn, converts=[w_ffn_gate[l], w_ffn_up[l]])
        w_gates = jnp.concatenate([w_rg_a[l], w_rg_x[l]], axis=-1).astype(BF16)
        mixed = mixer_epilogue(xg, attn, conv_w[l], conv_b[l], w_gates, b_rg_a[l], b_rg_x[l],
                               rg_lambda[l], g_attn_out[l], g_rnn_out[l], batch, seq)
        (h, hg, ssq), _ = matmul(mixed, w_out_l, F32, OUT_TILES, residual=h,
                                 out_gain=g_ffn[l], name="out_proj")

        (hidden,), (w_down_l,) = glu_matmul(hg, w_gate_l, w_up_l, ssq, converts=[w_ffn_down[l]])
        (h, hg, ssq), _ = matmul(hidden, w_down_l, F32, DOWN_TILES, residual=h,
                                 out_gain=g_ple[l], name="ffn_down")

        h = gated_embedding_add(hg, w_ple_l, ssq, h,
                                p[l].reshape(m, -1).astype(BF16), w_ple_proj[l].astype(BF16),
                                g_ple_out[l], g_final, final_norm=(l == depth - 1))
    return h.reshape(batch, seq, d_model)
```

```python
import functools

import jax
import jax.numpy as jnp
from jax import lax
from jax.experimental import pallas as pl
from jax.experimental.pallas import tpu as pltpu

F32 = jnp.float32
BF16 = jnp.bfloat16

NORM_EPS = 1e-6
D_HEAD = 128
RGLRU_C = 8.0
LOG2_E = 1.4426950408889634
MIN_LOG2_WEIGHT = -150.0

LANES = 128
SUBLANES = 8
VMEM_LIMIT_BYTES = 60 * 1024 * 1024

PROJ_TILES = (1024, 1024)
OUT_TILES = (1024, 1024)
GLU_TILES = (2048, 256)
DOWN_TILES = (512, 1024)
PLE_TILES = (512, 512)
NORM_ROWS = 256
ATTN_ROWS = 256
ATTN_HEADS = 8
SCAN_ROWS = 256
EPILOGUE_ROWS = 256


def _params(*semantics):
    return pltpu.CompilerParams(dimension_semantics=semantics,
                                vmem_limit_bytes=VMEM_LIMIT_BYTES)


def _tile(n, want):
    t = min(n, want)
    assert n % t == 0, (n, want)
    return t


def _rms(x, g):
    y = x * lax.rsqrt(jnp.mean(x * x, axis=-1, keepdims=True) + NORM_EPS)
    return y * g


def _rmsnorm_kernel(x_ref, g_ref, o_ref):
    o_ref[...] = _rms(x_ref[...], g_ref[...]).astype(o_ref.dtype)


def rmsnorm(x, g, out_dtype):
    m, d = x.shape
    tm = _tile(m, NORM_ROWS)
    return pl.pallas_call(
        _rmsnorm_kernel,
        grid=(m // tm,),
        in_specs=[pl.BlockSpec((tm, d), lambda i: (i, 0)),
                  pl.BlockSpec((1, d), lambda i: (0, 0))],
        out_specs=pl.BlockSpec((tm, d), lambda i: (i, 0)),
        out_shape=jax.ShapeDtypeStruct((m, d), out_dtype),
        compiler_params=_params("parallel"),
        name="rmsnorm",
    )(x, g.reshape(1, d))


def _row_chunks(rows):
    ch = min(rows, EPILOGUE_ROWS)
    assert rows % ch == 0
    return [pl.ds(c * ch, ch) for c in range(rows // ch)]


def _inv_rms(ssq_ref, rows, d):
    parts = ssq_ref.shape[1] // LANES
    total = ssq_ref[rows, 0:1]
    for p in range(1, parts):
        total = total + ssq_ref[rows, p * LANES:p * LANES + 1]
    return lax.rsqrt(total * (1.0 / d) + NORM_EPS)


def _emit_scaled(h, rows, g_ref, hg_ref, ssq_ref):
    hg_ref[rows, :] = (h * g_ref[...]).astype(hg_ref.dtype)
    part = jnp.sum(h * h, axis=1, keepdims=True)
    ssq_ref[rows, :] = jnp.broadcast_to(part, (h.shape[0], ssq_ref.shape[1]))


class _Side:
    def __init__(self, w, n_steps, linear_step):
        rows, cols = w.shape
        units = rows // (2 * SUBLANES)
        assert units * 2 * SUBLANES == rows
        n_chunks = max(c for c in range(1, min(units, n_steps) + 1) if units % c == 0)
        chunk = rows // n_chunks
        index = lambda *g: (jnp.minimum(linear_step(*g), n_chunks - 1), 0)
        self.w = w
        self.in_spec = pl.BlockSpec((chunk, cols), index)
        self.out_spec = pl.BlockSpec((chunk, cols), index)
        self.out_shape = jax.ShapeDtypeStruct((rows, cols), BF16)


def _with_sides(body, n_in, n_out, n_side):
    if n_side == 0:
        return body

    def wrapped(*refs):
        side_in = refs[n_in:n_in + n_side]
        o0 = n_in + n_side
        side_out = refs[o0 + n_out:o0 + n_out + n_side]
        for src, dst in zip(side_in, side_out):
            dst[...] = src[...].astype(dst.dtype)
        body(*refs[:n_in], *refs[o0:o0 + n_out], *refs[o0 + n_out + n_side:])

    return wrapped


def _call(body, grid, in_specs, out_specs, out_shapes, args, sides, semantics, name,
          scratch_shapes=()):
    n_in, n_out = len(in_specs), len(out_specs)
    outs = pl.pallas_call(
        _with_sides(body, n_in, n_out, len(sides)),
        grid=grid,
        in_specs=list(in_specs) + [s.in_spec for s in sides],
        out_specs=list(out_specs) + [s.out_spec for s in sides],
        out_shape=list(out_shapes) + [s.out_shape for s in sides],
        scratch_shapes=list(scratch_shapes),
        compiler_params=_params(*semantics),
        name=name,
    )(*args, *[s.w for s in sides])
    return outs[:n_out], outs[n_out:]


def _mm_kernel(*refs, has_res, d_scale, has_gain):
    refs = list(refs)
    a_ref, w_ref = refs.pop(0), refs.pop(0)
    r_ref = refs.pop(0) if has_res else None
    s_ref = refs.pop(0) if d_scale else None
    g_ref = refs.pop(0) if has_gain else None
    o_ref = refs.pop(0)
    if has_gain:
        hg_ref, ssq_ref = refs
    for rows in _row_chunks(a_ref.shape[0]):
        acc = jnp.dot(a_ref[rows, :], w_ref[...], preferred_element_type=F32)
        if d_scale:
            acc = acc * _inv_rms(s_ref, rows, d_scale)
        if has_res:
            acc = r_ref[rows, :] + acc
        o_ref[rows, :] = acc.astype(o_ref.dtype)
        if has_gain:
            _emit_scaled(acc, rows, g_ref, hg_ref, ssq_ref)


def matmul(a, w, out_dtype, tiles, *, name, residual=None, col_start=0, n=None,
           in_ssq=None, out_gain=None, converts=(), weight_major=False):
    m, k = a.shape
    n = w.shape[1] if n is None else n
    tm, tn = _tile(m, tiles[0]), _tile(n, tiles[1])
    assert col_start % tn == 0
    j0 = col_start // tn
    gi, gj = m // tm, n // tn
    if weight_major:
        grid = (gj, gi)
        ij = lambda f: (lambda j, i: f(i, j))
        w_mode = dict(pipeline_mode=pl.Buffered(1))
        step = lambda j, i: j * gi + i
    else:
        grid = (gi, gj)
        ij = lambda f: f
        w_mode = {}
        step = lambda i, j: i * gj + j
    in_specs = [pl.BlockSpec((tm, k), ij(lambda i, j: (i, 0))),
                pl.BlockSpec((k, tn), ij(lambda i, j: (0, j0 + j)), **w_mode)]
    args = [a, w]
    if residual is not None:
        in_specs.append(pl.BlockSpec((tm, tn), ij(lambda i, j: (i, j))))
        args.append(residual)
    if in_ssq is not None:
        in_specs.append(pl.BlockSpec((tm, in_ssq.shape[1]), ij(lambda i, j: (i, 0))))
        args.append(in_ssq)
    out_specs = [pl.BlockSpec((tm, tn), ij(lambda i, j: (i, j)))]
    out_shapes = [jax.ShapeDtypeStruct((m, n), out_dtype)]
    if out_gain is not None:
        in_specs.append(pl.BlockSpec((1, tn), ij(lambda i, j: (0, j))))
        args.append(out_gain.reshape(1, n))
        out_specs += [pl.BlockSpec((tm, tn), ij(lambda i, j: (i, j))),
                      pl.BlockSpec((tm, LANES), ij(lambda i, j: (i, j)))]
        out_shapes += [jax.ShapeDtypeStruct((m, n), BF16),
                       jax.ShapeDtypeStruct((m, gj * LANES), F32)]
    body = functools.partial(_mm_kernel, has_res=residual is not None,
                             d_scale=k if in_ssq is not None else 0,
                             has_gain=out_gain is not None)
    sides = [_Side(c, gi * gj, step) for c in converts]
    return _call(body, grid, in_specs, out_specs, out_shapes, args, sides,
                 ("arbitrary", "arbitrary"), name)


def _glu_kernel(a_ref, wg_ref, wu_ref, s_ref, o_ref, *, d_scale):
    for rows in _row_chunks(a_ref.shape[0]):
        a = a_ref[rows, :]
        inv = _inv_rms(s_ref, rows, d_scale)
        g = jnp.dot(a, wg_ref[...], preferred_element_type=F32) * inv
        u = jnp.dot(a, wu_ref[...], preferred_element_type=F32) * inv
        o_ref[rows, :] = (g * jax.nn.sigmoid(g) * u).astype(o_ref.dtype)


def glu_matmul(a, wg, wu, in_ssq, converts=()):
    m, k = a.shape
    n = wg.shape[1]
    tm, tn = _tile(m, GLU_TILES[0]), _tile(n, GLU_TILES[1])
    gj = n // tn
    sides = [_Side(c, (m // tm) * gj, lambda i, j: i * gj + j) for c in converts]
    return _call(
        functools.partial(_glu_kernel, d_scale=k), (m // tm, gj),
        [pl.BlockSpec((tm, k), lambda i, j: (i, 0)),
         pl.BlockSpec((k, tn), lambda i, j: (0, j)),
         pl.BlockSpec((k, tn), lambda i, j: (0, j)),
         pl.BlockSpec((tm, in_ssq.shape[1]), lambda i, j: (i, 0))],
        [pl.BlockSpec((tm, tn), lambda i, j: (i, j))],
        [jax.ShapeDtypeStruct((m, n), BF16)],
        [a, wg, wu, in_ssq], sides, ("parallel", "arbitrary"), "ffn_gate_up")


def _ple_kernel(a_ref, w_ref, s_ref, h_ref, p_ref, wp_ref, gp_ref, gf_ref, o_ref, pe_ref, *,
                tn, d_scale, final_norm):
    j = pl.program_id(1)

    @pl.when(j == 0)
    def _():
        pe_ref[...] = _rms(jnp.dot(p_ref[...], wp_ref[...], preferred_element_type=F32),
                           gp_ref[...])

    cols = pl.ds(pl.multiple_of(j * tn, tn), tn)
    for rows in _row_chunks(a_ref.shape[0]):
        z = (jnp.dot(a_ref[rows, :], w_ref[...], preferred_element_type=F32)
             * _inv_rms(s_ref, rows, d_scale))
        o_ref[rows, cols] = h_ref[rows, :] + jax.nn.sigmoid(z) * pe_ref[rows, cols]

    if final_norm:
        @pl.when(j == pl.num_programs(1) - 1)
        def _():
            o_ref[...] = _rms(o_ref[...], gf_ref[...])


def gated_embedding_add(a, w, in_ssq, h, p, wp, gp, gf, final_norm):
    m, k = a.shape
    n = w.shape[1]
    kp = p.shape[1]
    tm, tn = _tile(m, PLE_TILES[0]), _tile(n, PLE_TILES[1])
    kern = functools.partial(_ple_kernel, tn=tn, d_scale=k, final_norm=final_norm)
    return pl.pallas_call(
        kern,
        grid=(m // tm, n // tn),
        in_specs=[pl.BlockSpec((tm, k), lambda i, j: (i, 0)),
                  pl.BlockSpec((k, tn), lambda i, j: (0, j)),
                  pl.BlockSpec((tm, in_ssq.shape[1]), lambda i, j: (i, 0)),
                  pl.BlockSpec((tm, tn), lambda i, j: (i, j)),
                  pl.BlockSpec((tm, kp), lambda i, j: (i, 0)),
                  pl.BlockSpec((kp, n), lambda i, j: (0, 0)),
                  pl.BlockSpec((1, n), lambda i, j: (0, 0)),
                  pl.BlockSpec((1, n), lambda i, j: (0, 0))],
        out_specs=pl.BlockSpec((tm, n), lambda i, j: (i, 0)),
        out_shape=jax.ShapeDtypeStruct((m, n), F32),
        scratch_shapes=[pltpu.VMEM((tm, n), F32)],
        compiler_params=_params("parallel", "arbitrary"),
        name="ple_gate",
    )(a, w, in_ssq, h, p, wp, gp.reshape(1, n), gf.reshape(1, n))


def _suffix_sum_matrix(n):
    r = lax.broadcasted_iota(jnp.int32, (2 * n, n), 0) % n
    c = lax.broadcasted_iota(jnp.int32, (2 * n, n), 1)
    return jnp.where(r >= c, 1.0, 0.0).astype(BF16)


def _attn_kernel(q_ref, k_ref, v_ref, t_ref, o_ref, acc_ref, carry_ref, *, tq, heads, scale):
    qi = pl.program_id(2)
    groups = tq // LANES
    sign_bit = jnp.uint32(0x80000000)

    def all_heads(kstart, masked):
        hs = range(heads)
        cols = [slice(h * D_HEAD, (h + 1) * D_HEAD) for h in hs]
        if masked:
            row = lax.broadcasted_iota(jnp.int32, (tq, tq), 0)
            col = lax.broadcasted_iota(jnp.int32, (tq, tq), 1)
            causal = col < row
        nz = [lax.dot_general(q_ref[:, cols[h]], k_ref[pl.ds(kstart, tq), cols[h]],
                              (((1,), (1,)), ((), ())), preferred_element_type=F32)
              * (-scale * LOG2_E) for h in hs]
        sums = []
        for h in hs:
            neg_abs = lax.bitcast_convert_type(
                lax.bitcast_convert_type(nz[h], jnp.uint32) | sign_bit, F32)
            log_keep = jnp.minimum(nz[h], 0.0) - jnp.log(1.0 + jnp.exp2(neg_abs)) * LOG2_E
            if masked:
                log_keep = jnp.where(causal, log_keep, 0.0)
            hi = log_keep.astype(BF16)
            lo = (log_keep - hi.astype(F32)).astype(BF16)
            sums.append(jnp.dot(jnp.concatenate([hi, lo], axis=1), t_ref[...],
                                preferred_element_type=F32))
        top = None
        for h in hs:
            carry = carry_ref[h]
            w = jnp.exp2(sums[h] - nz[h] + jnp.concatenate([carry] * groups, axis=1))
            if masked:
                w = jnp.where(causal, w, 0.0)
            acc_ref[h] += jnp.dot(w.astype(BF16), v_ref[pl.ds(kstart, tq), cols[h]],
                                  preferred_element_type=F32)
            carry = carry + jnp.broadcast_to(sums[h][:, 0:1], (tq, LANES))
            carry_ref[h] = carry
            m = jnp.max(carry)
            top = m if top is None else jnp.maximum(top, m)
        return top

    for h in range(heads):
        carry_ref[h] = jnp.zeros((tq, LANES), F32)
        acc_ref[h] = jnp.zeros((tq, D_HEAD), F32)
    top = all_heads(pl.multiple_of(qi * tq, tq), True)

    def more(c):
        j, top = c
        return jnp.logical_and(j < qi, top >= MIN_LOG2_WEIGHT)

    def body(c):
        j, _ = c
        return j + 1, all_heads(pl.multiple_of((qi - 1 - j) * tq, tq), False)

    lax.while_loop(more, body, (jnp.int32(0), top))

    for h in range(heads):
        o_ref[:, h * D_HEAD:(h + 1) * D_HEAD] = acc_ref[h]


def stick_breaking_attention(qkv, batch, seq, d_attn, converts=()):
    n_heads = d_attn // D_HEAD
    heads = min(ATTN_HEADS, n_heads)
    assert n_heads % heads == 0
    width = heads * D_HEAD
    pairs = d_attn // width
    tq = _tile(seq, ATTN_ROWS)
    nq = seq // tq
    kern = functools.partial(_attn_kernel, tq=tq, heads=heads, scale=D_HEAD ** -0.5)
    sides = [_Side(c, batch * pairs * nq, lambda b, p, i: (b * pairs + p) * nq + i)
             for c in converts]
    return _call(
        kern, (batch, pairs, nq),
        [pl.BlockSpec((tq, width), lambda b, p, i: (b * nq + i, p)),
         pl.BlockSpec((seq, width), lambda b, p, i: (b, pairs + p)),
         pl.BlockSpec((seq, width), lambda b, p, i: (b, 2 * pairs + p)),
         pl.BlockSpec((2 * tq, tq), lambda b, p, i: (0, 0))],
        [pl.BlockSpec((tq, width), lambda b, p, i: (b * nq + i, p))],
        [jax.ShapeDtypeStruct((batch * seq, d_attn), F32)],
        [qkv, qkv, qkv, _suffix_sum_matrix(tq)], sides,
        ("parallel", "parallel", "arbitrary"), "stick_breaking_attention",
        scratch_shapes=[pltpu.VMEM((heads, tq, D_HEAD), F32),
                        pltpu.VMEM((heads, tq, LANES), F32)])


def _softplus(x):
    return jnp.maximum(x, 0.0) + jnp.log1p(jnp.exp(-jnp.abs(x)))


def _mixer_kernel(xr_ref, gr_ref, attn_ref, cw_ref, cb_ref, wg_ref, ba_ref, bx_ref, lam_ref,
                  ga_ref, gn_ref, o_ref, tail_ref, h_ref, rec_ref, *, ts, n_blocks, conv_width):
    @pl.when(pl.program_id(1) == 0)
    def _():
        tail_ref[...] = jnp.zeros_like(tail_ref)
        h_ref[...] = jnp.zeros_like(h_ref)

    row8 = lax.broadcasted_iota(jnp.int32, (SUBLANES, LANES), 0)
    sub3 = lax.broadcasted_iota(jnp.int32, (ts // SUBLANES, SUBLANES, LANES), 1)

    def block(n, ssq):
        lanes = pl.ds(pl.multiple_of(n * LANES, LANES), LANES)
        x = xr_ref[:, lanes]
        tail = tail_ref[:, lanes]
        assert conv_width - 1 < SUBLANES
        y = cb_ref[:, lanes] + cw_ref[conv_width - 1:conv_width, lanes] * x
        x3 = x.reshape(ts // SUBLANES, SUBLANES, LANES)
        for d in range(1, conv_width):
            rot = pltpu.roll(x3, d, axis=1)
            prev = jnp.concatenate([pltpu.roll(tail, d, axis=0)[None], rot[:-1]], axis=0)
            xs = jnp.where(sub3 >= d, rot, prev).reshape(ts, LANES)
            y = y + cw_ref[conv_width - 1 - d:conv_width - d, lanes] * xs
        tail_ref[:, lanes] = x[ts - SUBLANES:]

        gates = jnp.dot(y.astype(BF16), wg_ref[n], preferred_element_type=F32)
        r = jax.nn.sigmoid(gates[:, :LANES] + ba_ref[:, lanes])
        i = jax.nn.sigmoid(gates[:, LANES:] + bx_ref[:, lanes])
        log_a = -RGLRU_C * r * _softplus(-lam_ref[:, lanes])
        a = jnp.exp(log_a)
        v = -jnp.tanh(log_a) * (a * a + 1.0)
        u = jnp.where(v > 0.0, v * lax.rsqrt(v), 0.0) * (i * y)

        a = a.reshape(ts // SUBLANES, SUBLANES, LANES)
        u = u.reshape(ts // SUBLANES, SUBLANES, LANES)
        for d in (1, 2, 4):
            keep = sub3 >= d
            a_prev = jnp.where(keep, pltpu.roll(a, d, axis=1), 1.0)
            u_prev = jnp.where(keep, pltpu.roll(u, d, axis=1), 0.0)
            u = a * u_prev + u
            a = a * a_prev
        a = a.reshape(ts, LANES)
        u = u.reshape(ts, LANES)
        h_prev = h_ref[:, lanes]
        hs = []
        for g in range(ts // SUBLANES):
            rows = slice(g * SUBLANES, (g + 1) * SUBLANES)
            hg = a[rows] * h_prev + u[rows]
            hs.append(hg)
            h_prev = jnp.broadcast_to(hg[SUBLANES - 1:SUBLANES], (SUBLANES, LANES))
        h_ref[:, lanes] = h_prev
        rec = jnp.concatenate(hs, axis=0) * jax.nn.gelu(gr_ref[:, lanes], approximate=True)
        rec_ref[:, lanes] = rec
        return ssq + jnp.sum(rec * rec, axis=1, keepdims=True)

    ssq = lax.fori_loop(0, n_blocks, block, jnp.zeros((ts, 1), F32),
                        unroll=2 if n_blocks % 2 == 0 else 1)
    d_attn = attn_ref.shape[1]
    d_rnn = n_blocks * LANES
    inv = lax.rsqrt(ssq / d_rnn + NORM_EPS)
    step = 2 * SUBLANES
    for r0 in range(0, ts, step):
        rows = pl.ds(r0, step)
        o_ref[rows, :d_attn] = _rms(attn_ref[rows, :], ga_ref[...]).astype(o_ref.dtype)
        o_ref[rows, d_attn:] = (rec_ref[rows, :] * inv[r0:r0 + step]
                                * gn_ref[...]).astype(o_ref.dtype)


def mixer_epilogue(xg, attn, conv_w, conv_b, w_gates, b_a, b_x, lam, g_attn, g_rnn, batch, seq):
    d_rnn = xg.shape[1] // 2
    d_attn = attn.shape[1]
    n_blocks = d_rnn // LANES
    assert w_gates.shape == (n_blocks, LANES, 2 * LANES)
    conv_width = conv_w.shape[0]
    ts = _tile(seq, SCAN_ROWS)
    ns = seq // ts
    row1 = lambda v: v.reshape(1, -1)
    const = lambda shape: pl.BlockSpec(shape, lambda b, s: (0,) * len(shape))
    kern = functools.partial(_mixer_kernel, ts=ts, n_blocks=n_blocks, conv_width=conv_width)
    return pl.pallas_call(
        kern,
        grid=(batch, ns),
        in_specs=[pl.BlockSpec((ts, d_rnn), lambda b, s: (b * ns + s, 0)),
                  pl.BlockSpec((ts, d_rnn), lambda b, s: (b * ns + s, 1)),
                  pl.BlockSpec((ts, d_attn), lambda b, s: (b * ns + s, 0)),
                  const((conv_width, d_rnn)), const((1, d_rnn)),
                  const((n_blocks, LANES, 2 * LANES)),
                  const((1, d_rnn)), const((1, d_rnn)), const((1, d_rnn)),
                  const((1, d_attn)), const((1, d_rnn))],
        out_specs=pl.BlockSpec((ts, d_attn + d_rnn), lambda b, s: (b * ns + s, 0)),
        out_shape=jax.ShapeDtypeStruct((batch * seq, d_attn + d_rnn), BF16),
        scratch_shapes=[pltpu.VMEM((SUBLANES, d_rnn), F32),
                        pltpu.VMEM((SUBLANES, d_rnn), F32),
                        pltpu.VMEM((ts, d_rnn), F32)],
        compiler_params=_params("parallel", "arbitrary"),
        name="rglru_mixer",
    )(xg, xg, attn, conv_w, row1(conv_b), w_gates, row1(b_a), row1(b_x), row1(lam),
      row1(g_attn), row1(g_rnn))


def kernel(x, p, g_mix, w_in, conv_w, conv_b, w_rg_a, b_rg_a, w_rg_x, b_rg_x, rg_lambda,
           g_attn_out, g_rnn_out, w_out, g_ffn, w_ffn_gate, w_ffn_up, w_ffn_down, g_ple,
           w_ple_gate, w_ple_proj, g_ple_out, g_final):
    batch, seq, d_model = x.shape
    depth = w_in.shape[0]
    d_attn = g_attn_out.shape[-1]
    d_rnn = g_rnn_out.shape[-1]
    m = batch * seq

    h = x.reshape(m, d_model)
    for l in range(depth):
        u = rmsnorm(h, g_mix[l], BF16)
        w_in_l = w_in[l].astype(BF16)
        (qkv,), (w_out_l,) = matmul(u, w_in_l, BF16, PROJ_TILES, name="in_proj_qkv",
                                    n=3 * d_attn, converts=[w_out[l]])
        (xg,), (w_ple_l,) = matmul(u, w_in_l, F32, PROJ_TILES, name="in_proj_rnn",
                                   col_start=3 * d_attn, n=2 * d_rnn, converts=[w_ple_gate[l]])
        (attn,), (w_gate_l, w_up_l) = stick_breaking_attention(
            qkv, batch, seq, d_attn, converts=[w_ffn_gate[l], w_ffn_up[l]])
        w_gates = jnp.concatenate([w_rg_a[l], w_rg_x[l]], axis=-1).astype(BF16)
        mixed = mixer_epilogue(xg, attn, conv_w[l], conv_b[l], w_gates, b_rg_a[l], b_rg_x[l],
                               rg_lambda[l], g_attn_out[l], g_rnn_out[l], batch, seq)
        (h, hg, ssq), _ = matmul(mixed, w_out_l, F32, OUT_TILES, residual=h,
                                 out_gain=g_ffn[l], name="out_proj")

        (hidden,), (w_down_l,) = glu_matmul(hg, w_gate_l, w_up_l, ssq, converts=[w_ffn_down[l]])
        (h, hg, ssq), _ = matmul(hidden, w_down_l, F32, DOWN_TILES, residual=h,
                                 out_gain=g_ple[l], name="ffn_down", weight_major=True)

        h = gated_embedding_add(hg, w_ple_l, ssq, h,
                                p[l].reshape(m, -1).astype(BF16), w_ple_proj[l].astype(BF16),
                                g_ple_out[l], g_final, final_norm=(l == depth - 1))
    return h.reshape(batch, seq, d_model)
```

```python
import functools

import jax
import jax.numpy as jnp
from jax import lax
from jax.experimental import pallas as pl
from jax.experimental.pallas import tpu as pltpu

F32 = jnp.float32
BF16 = jnp.bfloat16

NORM_EPS = 1e-6
D_HEAD = 128
RGLRU_C = 8.0
LOG2_E = 1.4426950408889634
MIN_LOG2_WEIGHT = -150.0
F32_TINY = 1.1754943508222875e-38

LANES = 128
SUBLANES = 8
VMEM_LIMIT_BYTES = 60 * 1024 * 1024

PROJ_TILES = (1024, 1024)
OUT_TILES = (1024, 1024)
GLU_TILES = (2048, 256)
DOWN_TILES = (512, 1024)
PLE_TILES = (512, 512)
NORM_ROWS = 256
ATTN_ROWS = 256
ATTN_HEADS = 8
SCAN_ROWS = 256
EPILOGUE_ROWS = 256


def _params(*semantics):
    return pltpu.CompilerParams(dimension_semantics=semantics,
                                vmem_limit_bytes=VMEM_LIMIT_BYTES)


def _tile(n, want):
    t = min(n, want)
    assert n % t == 0, (n, want)
    return t


def _rms(x, g):
    y = x * lax.rsqrt(jnp.mean(x * x, axis=-1, keepdims=True) + NORM_EPS)
    return y * g


def _rmsnorm_kernel(x_ref, g_ref, o_ref):
    o_ref[...] = _rms(x_ref[...], g_ref[...]).astype(o_ref.dtype)


def rmsnorm(x, g, out_dtype):
    m, d = x.shape
    tm = _tile(m, NORM_ROWS)
    return pl.pallas_call(
        _rmsnorm_kernel,
        grid=(m // tm,),
        in_specs=[pl.BlockSpec((tm, d), lambda i: (i, 0)),
                  pl.BlockSpec((1, d), lambda i: (0, 0))],
        out_specs=pl.BlockSpec((tm, d), lambda i: (i, 0)),
        out_shape=jax.ShapeDtypeStruct((m, d), out_dtype),
        compiler_params=_params("parallel"),
        name="rmsnorm",
    )(x, g.reshape(1, d))


def _row_chunks(rows):
    ch = min(rows, EPILOGUE_ROWS)
    assert rows % ch == 0
    return [pl.ds(c * ch, ch) for c in range(rows // ch)]


def _inv_rms(ssq_ref, rows, d):
    parts = ssq_ref.shape[1] // LANES
    total = ssq_ref[rows, 0:1]
    for p in range(1, parts):
        total = total + ssq_ref[rows, p * LANES:p * LANES + 1]
    return lax.rsqrt(total * (1.0 / d) + NORM_EPS)


def _emit_scaled(h, rows, g_ref, hg_ref, ssq_ref):
    hg_ref[rows, :] = (h * g_ref[...]).astype(hg_ref.dtype)
    part = jnp.sum(h * h, axis=1, keepdims=True)
    ssq_ref[rows, :] = jnp.broadcast_to(part, (h.shape[0], ssq_ref.shape[1]))


class _Side:
    def __init__(self, w, n_steps, linear_step):
        rows, cols = w.shape
        units = rows // (2 * SUBLANES)
        assert units * 2 * SUBLANES == rows
        n_chunks = max(c for c in range(1, min(units, n_steps) + 1) if units % c == 0)
        chunk = rows // n_chunks
        index = lambda *g: (jnp.minimum(linear_step(*g), n_chunks - 1), 0)
        self.w = w
        self.in_spec = pl.BlockSpec((chunk, cols), index)
        self.out_spec = pl.BlockSpec((chunk, cols), index)
        self.out_shape = jax.ShapeDtypeStruct((rows, cols), BF16)


def _with_sides(body, n_in, n_out, n_side):
    if n_side == 0:
        return body

    def wrapped(*refs):
        side_in = refs[n_in:n_in + n_side]
        o0 = n_in + n_side
        side_out = refs[o0 + n_out:o0 + n_out + n_side]
        for src, dst in zip(side_in, side_out):
            dst[...] = src[...].astype(dst.dtype)
        body(*refs[:n_in], *refs[o0:o0 + n_out], *refs[o0 + n_out + n_side:])

    return wrapped


def _call(body, grid, in_specs, out_specs, out_shapes, args, sides, semantics, name,
          scratch_shapes=()):
    n_in, n_out = len(in_specs), len(out_specs)
    outs = pl.pallas_call(
        _with_sides(body, n_in, n_out, len(sides)),
        grid=grid,
        in_specs=list(in_specs) + [s.in_spec for s in sides],
        out_specs=list(out_specs) + [s.out_spec for s in sides],
        out_shape=list(out_shapes) + [s.out_shape for s in sides],
        scratch_shapes=list(scratch_shapes),
        compiler_params=_params(*semantics),
        name=name,
    )(*args, *[s.w for s in sides])
    return outs[:n_out], outs[n_out:]


def _mm_kernel(*refs, has_res, d_scale, has_gain):
    refs = list(refs)
    a_ref, w_ref = refs.pop(0), refs.pop(0)
    r_ref = refs.pop(0) if has_res else None
    s_ref = refs.pop(0) if d_scale else None
    g_ref = refs.pop(0) if has_gain else None
    o_ref = refs.pop(0)
    if has_gain:
        hg_ref, ssq_ref = refs
    for rows in _row_chunks(a_ref.shape[0]):
        acc = jnp.dot(a_ref[rows, :], w_ref[...], preferred_element_type=F32)
        if d_scale:
            acc = acc * _inv_rms(s_ref, rows, d_scale)
        if has_res:
            acc = r_ref[rows, :] + acc
        o_ref[rows, :] = acc.astype(o_ref.dtype)
        if has_gain:
            _emit_scaled(acc, rows, g_ref, hg_ref, ssq_ref)


def matmul(a, w, out_dtype, tiles, *, name, residual=None, col_start=0, n=None,
           in_ssq=None, out_gain=None, converts=(), weight_major=False):
    m, k = a.shape
    n = w.shape[1] if n is None else n
    tm, tn = _tile(m, tiles[0]), _tile(n, tiles[1])
    assert col_start % tn == 0
    j0 = col_start // tn
    gi, gj = m // tm, n // tn
    if weight_major:
        grid = (gj, gi)
        ij = lambda f: (lambda j, i: f(i, j))
        w_mode = dict(pipeline_mode=pl.Buffered(1))
        step = lambda j, i: j * gi + i
    else:
        grid = (gi, gj)
        ij = lambda f: f
        w_mode = {}
        step = lambda i, j: i * gj + j
    in_specs = [pl.BlockSpec((tm, k), ij(lambda i, j: (i, 0))),
                pl.BlockSpec((k, tn), ij(lambda i, j: (0, j0 + j)), **w_mode)]
    args = [a, w]
    if residual is not None:
        in_specs.append(pl.BlockSpec((tm, tn), ij(lambda i, j: (i, j))))
        args.append(residual)
    if in_ssq is not None:
        in_specs.append(pl.BlockSpec((tm, in_ssq.shape[1]), ij(lambda i, j: (i, 0))))
        args.append(in_ssq)
    out_specs = [pl.BlockSpec((tm, tn), ij(lambda i, j: (i, j)))]
    out_shapes = [jax.ShapeDtypeStruct((m, n), out_dtype)]
    if out_gain is not None:
        in_specs.append(pl.BlockSpec((1, tn), ij(lambda i, j: (0, j))))
        args.append(out_gain.reshape(1, n))
        out_specs += [pl.BlockSpec((tm, tn), ij(lambda i, j: (i, j))),
                      pl.BlockSpec((tm, LANES), ij(lambda i, j: (i, j)))]
        out_shapes += [jax.ShapeDtypeStruct((m, n), BF16),
                       jax.ShapeDtypeStruct((m, gj * LANES), F32)]
    body = functools.partial(_mm_kernel, has_res=residual is not None,
                             d_scale=k if in_ssq is not None else 0,
                             has_gain=out_gain is not None)
    sides = [_Side(c, gi * gj, step) for c in converts]
    return _call(body, grid, in_specs, out_specs, out_shapes, args, sides,
                 ("arbitrary", "arbitrary"), name)


def _glu_kernel(a_ref, wg_ref, wu_ref, s_ref, o_ref, *, d_scale):
    for rows in _row_chunks(a_ref.shape[0]):
        a = a_ref[rows, :]
        inv = _inv_rms(s_ref, rows, d_scale)
        g = jnp.dot(a, wg_ref[...], preferred_element_type=F32) * inv
        u = jnp.dot(a, wu_ref[...], preferred_element_type=F32) * inv
        o_ref[rows, :] = (g * jax.nn.sigmoid(g) * u).astype(o_ref.dtype)


def glu_matmul(a, wg, wu, in_ssq, converts=()):
    m, k = a.shape
    n = wg.shape[1]
    tm, tn = _tile(m, GLU_TILES[0]), _tile(n, GLU_TILES[1])
    gj = n // tn
    sides = [_Side(c, (m // tm) * gj, lambda i, j: i * gj + j) for c in converts]
    return _call(
        functools.partial(_glu_kernel, d_scale=k), (m // tm, gj),
        [pl.BlockSpec((tm, k), lambda i, j: (i, 0)),
         pl.BlockSpec((k, tn), lambda i, j: (0, j)),
         pl.BlockSpec((k, tn), lambda i, j: (0, j)),
         pl.BlockSpec((tm, in_ssq.shape[1]), lambda i, j: (i, 0))],
        [pl.BlockSpec((tm, tn), lambda i, j: (i, j))],
        [jax.ShapeDtypeStruct((m, n), BF16)],
        [a, wg, wu, in_ssq], sides, ("parallel", "arbitrary"), "ffn_gate_up")


def _ple_kernel(a_ref, w_ref, s_ref, h_ref, p_ref, wp_ref, gp_ref, gf_ref, o_ref, pe_ref, ssq_ref, *,
                tn, d_scale, final_norm):
    j = pl.program_id(1)
    tm, n = o_ref.shape

    @pl.when(j == 0)
    def _():
        pe_ref[...] = _rms(jnp.dot(p_ref[...], wp_ref[...], preferred_element_type=F32),
                           gp_ref[...])
        ssq_ref[...] = jnp.zeros_like(ssq_ref)

    cols = pl.ds(pl.multiple_of(j * tn, tn), tn)
    for rows in _row_chunks(tm):
        z = (jnp.dot(a_ref[rows, :], w_ref[...], preferred_element_type=F32)
             * _inv_rms(s_ref, rows, d_scale))
        out = h_ref[rows, :] + jax.nn.sigmoid(z) * pe_ref[rows, cols]
        o_ref[rows, cols] = out
        if final_norm:
            part = jnp.sum(out * out, axis=1, keepdims=True)
            ssq_ref[rows, :] += jnp.broadcast_to(part, (out.shape[0], LANES))

    if final_norm:
        @pl.when(j == pl.num_programs(1) - 1)
        def _():
            step = 2 * SUBLANES
            for r0 in range(0, tm, step):
                rows = pl.ds(r0, step)
                o_ref[rows, :] = o_ref[rows, :] * _inv_rms(ssq_ref, rows, n) * gf_ref[...]


def gated_embedding_add(a, w, in_ssq, h, p, wp, gp, gf, final_norm):
    m, k = a.shape
    n = w.shape[1]
    kp = p.shape[1]
    tm, tn = _tile(m, PLE_TILES[0]), _tile(n, PLE_TILES[1])
    kern = functools.partial(_ple_kernel, tn=tn, d_scale=k, final_norm=final_norm)
    return pl.pallas_call(
        kern,
        grid=(m // tm, n // tn),
        in_specs=[pl.BlockSpec((tm, k), lambda i, j: (i, 0)),
                  pl.BlockSpec((k, tn), lambda i, j: (0, j)),
                  pl.BlockSpec((tm, in_ssq.shape[1]), lambda i, j: (i, 0)),
                  pl.BlockSpec((tm, tn), lambda i, j: (i, j)),
                  pl.BlockSpec((tm, kp), lambda i, j: (i, 0)),
                  pl.BlockSpec((kp, n), lambda i, j: (0, 0)),
                  pl.BlockSpec((1, n), lambda i, j: (0, 0)),
                  pl.BlockSpec((1, n), lambda i, j: (0, 0))],
        out_specs=pl.BlockSpec((tm, n), lambda i, j: (i, 0)),
        out_shape=jax.ShapeDtypeStruct((m, n), F32),
        scratch_shapes=[pltpu.VMEM((tm, n), F32), pltpu.VMEM((tm, LANES), F32)],
        compiler_params=_params("parallel", "arbitrary"),
        name="ple_gate",
    )(a, w, in_ssq, h, p, wp, gp.reshape(1, n), gf.reshape(1, n))


def _suffix_sum_matrix(n):
    r = lax.broadcasted_iota(jnp.int32, (2 * n, n), 0) % n
    c = lax.broadcasted_iota(jnp.int32, (2 * n, n), 1)
    return jnp.where(r >= c, 1.0, 0.0).astype(BF16)


def _attn_kernel(q_ref, k_ref, v_ref, t_ref, o_ref, acc_ref, carry_ref, *, tq, heads, scale):
    qi = pl.program_id(2)
    groups = tq // LANES
    sign_bit = jnp.uint32(0x80000000)

    def all_heads(kstart, masked):
        hs = range(heads)
        cols = [slice(h * D_HEAD, (h + 1) * D_HEAD) for h in hs]
        if masked:
            row = lax.broadcasted_iota(jnp.int32, (tq, tq), 0)
            col = lax.broadcasted_iota(jnp.int32, (tq, tq), 1)
            causal = col < row
        nz = [lax.dot_general(q_ref[:, cols[h]], k_ref[pl.ds(kstart, tq), cols[h]],
                              (((1,), (1,)), ((), ())), preferred_element_type=F32)
              * (-scale * LOG2_E) for h in hs]
        sums = []
        for h in hs:
            neg_abs = lax.bitcast_convert_type(
                lax.bitcast_convert_type(nz[h], jnp.uint32) | sign_bit, F32)
            log_keep = jnp.minimum(nz[h], 0.0) - jnp.log(1.0 + jnp.exp2(neg_abs)) * LOG2_E
            if masked:
                log_keep = jnp.where(causal, log_keep, 0.0)
            hi = log_keep.astype(BF16)
            lo = (log_keep - hi.astype(F32)).astype(BF16)
            sums.append(jnp.dot(jnp.concatenate([hi, lo], axis=1), t_ref[...],
                                preferred_element_type=F32))
        top = None
        for h in hs:
            x = sums[h] - nz[h]
            total = jnp.broadcast_to(sums[h][:, 0:1], (tq, LANES))
            vc = v_ref[pl.ds(kstart, tq), cols[h]]
            if masked:
                w = jnp.where(causal, jnp.exp2(x), 0.0)
                acc_ref[h] = jnp.dot(w.astype(BF16), vc, preferred_element_type=F32)
                carry_ref[h] = total
            else:
                carry = carry_ref[h]
                w = jnp.exp2(x + jnp.concatenate([carry] * groups, axis=1))
                acc_ref[h] += jnp.dot(w.astype(BF16), vc, preferred_element_type=F32)
                carry = carry + total
                carry_ref[h] = carry
                m = jnp.max(carry)
                top = m if top is None else jnp.maximum(top, m)
        return top

    all_heads(pl.multiple_of(qi * tq, tq), True)
    top = jnp.float32(0.0)

    def more(c):
        j, top = c
        return jnp.logical_and(j < qi, top >= MIN_LOG2_WEIGHT)

    def body(c):
        j, _ = c
        return j + 1, all_heads(pl.multiple_of((qi - 1 - j) * tq, tq), False)

    lax.while_loop(more, body, (jnp.int32(0), top))

    for h in range(heads):
        o_ref[:, h * D_HEAD:(h + 1) * D_HEAD] = acc_ref[h]


def stick_breaking_attention(qkv, batch, seq, d_attn, converts=()):
    n_heads = d_attn // D_HEAD
    heads = min(ATTN_HEADS, n_heads)
    assert n_heads % heads == 0
    width = heads * D_HEAD
    pairs = d_attn // width
    tq = _tile(seq, ATTN_ROWS)
    nq = seq // tq
    kern = functools.partial(_attn_kernel, tq=tq, heads=heads, scale=D_HEAD ** -0.5)
    sides = [_Side(c, batch * pairs * nq, lambda b, p, i: (b * pairs + p) * nq + i)
             for c in converts]
    return _call(
        kern, (batch, pairs, nq),
        [pl.BlockSpec((tq, width), lambda b, p, i: (b * nq + i, p)),
         pl.BlockSpec((seq, width), lambda b, p, i: (b, pairs + p)),
         pl.BlockSpec((seq, width), lambda b, p, i: (b, 2 * pairs + p)),
         pl.BlockSpec((2 * tq, tq), lambda b, p, i: (0, 0))],
        [pl.BlockSpec((tq, width), lambda b, p, i: (b * nq + i, p))],
        [jax.ShapeDtypeStruct((batch * seq, d_attn), F32)],
        [qkv, qkv, qkv, _suffix_sum_matrix(tq)], sides,
        ("parallel", "parallel", "arbitrary"), "stick_breaking_attention",
        scratch_shapes=[pltpu.VMEM((heads, tq, D_HEAD), F32),
                        pltpu.VMEM((heads, tq, LANES), F32)])


def _softplus(x):
    return jnp.maximum(x, 0.0) + jnp.log1p(jnp.exp(-jnp.abs(x)))


def _mixer_kernel(xr_ref, gr_ref, attn_ref, cw_ref, cb_ref, wg_ref, ba_ref, bx_ref, lam_ref,
                  ga_ref, gn_ref, o_ref, tail_ref, h_ref, rec_ref, *, ts, n_blocks, conv_width):
    @pl.when(pl.program_id(1) == 0)
    def _():
        tail_ref[...] = jnp.zeros_like(tail_ref)
        h_ref[...] = jnp.zeros_like(h_ref)

    row8 = lax.broadcasted_iota(jnp.int32, (SUBLANES, LANES), 0)
    sub3 = lax.broadcasted_iota(jnp.int32, (ts // SUBLANES, SUBLANES, LANES), 1)

    def block(n, ssq):
        lanes = pl.ds(pl.multiple_of(n * LANES, LANES), LANES)
        x = xr_ref[:, lanes]
        tail = tail_ref[:, lanes]
        assert conv_width - 1 < SUBLANES
        y = cb_ref[:, lanes] + cw_ref[conv_width - 1:conv_width, lanes] * x
        x3 = x.reshape(ts // SUBLANES, SUBLANES, LANES)
        for d in range(1, conv_width):
            rot = pltpu.roll(x3, d, axis=1)
            prev = jnp.concatenate([pltpu.roll(tail, d, axis=0)[None], rot[:-1]], axis=0)
            xs = jnp.where(sub3 >= d, rot, prev).reshape(ts, LANES)
            y = y + cw_ref[conv_width - 1 - d:conv_width - d, lanes] * xs
        tail_ref[:, lanes] = x[ts - SUBLANES:]

        gates = jnp.dot(y.astype(BF16), wg_ref[n], preferred_element_type=F32)
        r = jax.nn.sigmoid(gates[:, :LANES] + ba_ref[:, lanes])
        i = jax.nn.sigmoid(gates[:, LANES:] + bx_ref[:, lanes])
        log_a = -RGLRU_C * r * _softplus(-lam_ref[:, lanes])
        a = jnp.exp(log_a)
        v = -jnp.tanh(log_a) * (a * a + 1.0)
        u = v * lax.rsqrt(jnp.maximum(v, F32_TINY)) * (i * y)

        a = a.reshape(ts // SUBLANES, SUBLANES, LANES)
        u = u.reshape(ts // SUBLANES, SUBLANES, LANES)
        for d in (1, 2, 4):
            keep = sub3 >= d
            a_prev = jnp.where(keep, pltpu.roll(a, d, axis=1), 1.0)
            u_prev = jnp.where(keep, pltpu.roll(u, d, axis=1), 0.0)
            u = a * u_prev + u
            a = a * a_prev
        a = a.reshape(ts, LANES)
        u = u.reshape(ts, LANES)
        h_prev = h_ref[:, lanes]
        hs = []
        for g in range(ts // SUBLANES):
            rows = slice(g * SUBLANES, (g + 1) * SUBLANES)
            hg = a[rows] * h_prev + u[rows]
            hs.append(hg)
            h_prev = jnp.broadcast_to(hg[SUBLANES - 1:SUBLANES], (SUBLANES, LANES))
        h_ref[:, lanes] = h_prev
        rec = jnp.concatenate(hs, axis=0) * jax.nn.gelu(gr_ref[:, lanes], approximate=True)
        rec_ref[:, lanes] = rec
        return ssq + jnp.sum(rec * rec, axis=1, keepdims=True)

    ssq = lax.fori_loop(0, n_blocks, block, jnp.zeros((ts, 1), F32),
                        unroll=2 if n_blocks % 2 == 0 else 1)
    d_attn = attn_ref.shape[1]
    d_rnn = n_blocks * LANES
    inv = lax.rsqrt(ssq / d_rnn + NORM_EPS)
    step = 2 * SUBLANES
    for r0 in range(0, ts, step):
        rows = pl.ds(r0, step)
        o_ref[rows, :d_attn] = _rms(attn_ref[rows, :], ga_ref[...]).astype(o_ref.dtype)
        o_ref[rows, d_attn:] = (rec_ref[rows, :] * inv[r0:r0 + step]
                                * gn_ref[...]).astype(o_ref.dtype)


def mixer_epilogue(xg, attn, conv_w, conv_b, w_gates, b_a, b_x, lam, g_attn, g_rnn, batch, seq):
    d_rnn = xg.shape[1] // 2
    d_attn = attn.shape[1]
    n_blocks = d_rnn // LANES
    assert w_gates.shape == (n_blocks, LANES, 2 * LANES)
    conv_width = conv_w.shape[0]
    ts = _tile(seq, SCAN_ROWS)
    ns = seq // ts
    row1 = lambda v: v.reshape(1, -1)
    const = lambda shape: pl.BlockSpec(shape, lambda b, s: (0,) * len(shape))
    kern = functools.partial(_mixer_kernel, ts=ts, n_blocks=n_blocks, conv_width=conv_width)
    return pl.pallas_call(
        kern,
        grid=(batch, ns),
        in_specs=[pl.BlockSpec((ts, d_rnn), lambda b, s: (b * ns + s, 0)),
                  pl.BlockSpec((ts, d_rnn), lambda b, s: (b * ns + s, 1)),
                  pl.BlockSpec((ts, d_attn), lambda b, s: (b * ns + s, 0)),
                  const((conv_width, d_rnn)), const((1, d_rnn)),
                  const((n_blocks, LANES, 2 * LANES)),
                  const((1, d_rnn)), const((1, d_rnn)), const((1, d_rnn)),
                  const((1, d_attn)), const((1, d_rnn))],
        out_specs=pl.BlockSpec((ts, d_attn + d_rnn), lambda b, s: (b * ns + s, 0)),
        out_shape=jax.ShapeDtypeStruct((batch * seq, d_attn + d_rnn), BF16),
        scratch_shapes=[pltpu.VMEM((SUBLANES, d_rnn), F32),
                        pltpu.VMEM((SUBLANES, d_rnn), F32),
                        pltpu.VMEM((ts, d_rnn), F32)],
        compiler_params=_params("parallel", "arbitrary"),
        name="rglru_mixer",
    )(xg, xg, attn, conv_w, row1(conv_b), w_gates, row1(b_a), row1(b_x), row1(lam),
      row1(g_attn), row1(g_rnn))


def kernel(x, p, g_mix, w_in, conv_w, conv_b, w_rg_a, b_rg_a, w_rg_x, b_rg_x, rg_lambda,
           g_attn_out, g_rnn_out, w_out, g_ffn, w_ffn_gate, w_ffn_up, w_ffn_down, g_ple,
           w_ple_gate, w_ple_proj, g_ple_out, g_final):
    batch, seq, d_model = x.shape
    depth = w_in.shape[0]
    d_attn = g_attn_out.shape[-1]
    d_rnn = g_rnn_out.shape[-1]
    m = batch * seq

    h = x.reshape(m, d_model)
    for l in range(depth):
        u = rmsnorm(h, g_mix[l], BF16)
        w_in_l = w_in[l].astype(BF16)
        (qkv,), (w_out_l,) = matmul(u, w_in_l, BF16, PROJ_TILES, name="in_proj_qkv",
                                    n=3 * d_attn, converts=[w_out[l]])
        (xg,), (w_ple_l,) = matmul(u, w_in_l, F32, PROJ_TILES, name="in_proj_rnn",
                                   col_start=3 * d_attn, n=2 * d_rnn, converts=[w_ple_gate[l]])
        (attn,), (w_gate_l, w_up_l) = stick_breaking_attention(
            qkv, batch, seq, d_attn, converts=[w_ffn_gate[l], w_ffn_up[l]])
        w_gates = jnp.concatenate([w_rg_a[l], w_rg_x[l]], axis=-1).astype(BF16)
        mixed = mixer_epilogue(xg, attn, conv_w[l], conv_b[l], w_gates, b_rg_a[l], b_rg_x[l],
                               rg_lambda[l], g_attn_out[l], g_rnn_out[l], batch, seq)
        (h, hg, ssq), _ = matmul(mixed, w_out_l, F32, OUT_TILES, residual=h,
                                 out_gain=g_ffn[l], name="out_proj")

        (hidden,), (w_down_l,) = glu_matmul(hg, w_gate_l, w_up_l, ssq, converts=[w_ffn_down[l]])
        (h, hg, ssq), _ = matmul(hidden, w_down_l, F32, DOWN_TILES, residual=h,
                                 out_gain=g_ple[l], name="ffn_down", weight_major=True)

        h = gated_embedding_add(hg, w_ple_l, ssq, h,
                                p[l].reshape(m, -1).astype(BF16), w_ple_proj[l].astype(BF16),
                                g_ple_out[l], g_final, final_norm=(l == depth - 1))
    return h.reshape(batch, seq, d_model)
```

```python
import functools

import jax
import jax.numpy as jnp
from jax import lax
from jax.experimental import pallas as pl
from jax.experimental.pallas import tpu as pltpu

F32 = jnp.float32
BF16 = jnp.bfloat16

NORM_EPS = 1e-6
D_HEAD = 128
RGLRU_C = 8.0
LOG2_E = 1.4426950408889634
MIN_LOG2_WEIGHT = -150.0
F32_TINY = 1.1754943508222875e-38

LANES = 128
SUBLANES = 8
VMEM_LIMIT_BYTES = 60 * 1024 * 1024

PROJ_TILES = (1024, 1024)
OUT_TILES = (1024, 1024)
GLU_TILES = (2048, 256)
DOWN_TILES = (512, 1024)
PLE_TILES = (512, 512)
NORM_ROWS = 256
ATTN_ROWS = 256
ATTN_HEADS = 8
ATTN_SPLIT_16THS = 11
SCAN_ROWS = 256
EPILOGUE_ROWS = 256


def _params(*semantics):
    return pltpu.CompilerParams(dimension_semantics=semantics,
                                vmem_limit_bytes=VMEM_LIMIT_BYTES)


def _tile(n, want):
    t = min(n, want)
    assert n % t == 0, (n, want)
    return t


def _rms(x, g):
    y = x * lax.rsqrt(jnp.mean(x * x, axis=-1, keepdims=True) + NORM_EPS)
    return y * g


def _rmsnorm_kernel(x_ref, g_ref, o_ref):
    o_ref[...] = _rms(x_ref[...], g_ref[...]).astype(o_ref.dtype)


def rmsnorm(x, g, out_dtype):
    m, d = x.shape
    tm = _tile(m, NORM_ROWS)
    return pl.pallas_call(
        _rmsnorm_kernel,
        grid=(m // tm,),
        in_specs=[pl.BlockSpec((tm, d), lambda i: (i, 0)),
                  pl.BlockSpec((1, d), lambda i: (0, 0))],
        out_specs=pl.BlockSpec((tm, d), lambda i: (i, 0)),
        out_shape=jax.ShapeDtypeStruct((m, d), out_dtype),
        compiler_params=_params("parallel"),
        name="rmsnorm",
    )(x, g.reshape(1, d))


def _row_chunks(rows):
    ch = min(rows, EPILOGUE_ROWS)
    assert rows % ch == 0
    return [pl.ds(c * ch, ch) for c in range(rows // ch)]


def _inv_rms(ssq_ref, rows, d):
    parts = ssq_ref.shape[1] // LANES
    total = ssq_ref[rows, 0:1]
    for p in range(1, parts):
        total = total + ssq_ref[rows, p * LANES:p * LANES + 1]
    return lax.rsqrt(total * (1.0 / d) + NORM_EPS)


def _emit_scaled(h, rows, g_ref, hg_ref, ssq_ref):
    hg_ref[rows, :] = (h * g_ref[...]).astype(hg_ref.dtype)
    part = jnp.sum(h * h, axis=1, keepdims=True)
    ssq_ref[rows, :] = jnp.broadcast_to(part, (h.shape[0], ssq_ref.shape[1]))


class _Side:
    def __init__(self, w, n_steps, linear_step):
        rows, cols = w.shape
        units = rows // (2 * SUBLANES)
        assert units * 2 * SUBLANES == rows
        n_chunks = max(c for c in range(1, min(units, n_steps) + 1) if units % c == 0)
        chunk = rows // n_chunks
        index = lambda *g: (jnp.minimum(linear_step(*g), n_chunks - 1), 0)
        self.w = w
        self.in_spec = pl.BlockSpec((chunk, cols), index)
        self.out_spec = pl.BlockSpec((chunk, cols), index)
        self.out_shape = jax.ShapeDtypeStruct((rows, cols), BF16)


def _with_sides(body, n_in, n_out, n_side):
    if n_side == 0:
        return body

    def wrapped(*refs):
        side_in = refs[n_in:n_in + n_side]
        o0 = n_in + n_side
        side_out = refs[o0 + n_out:o0 + n_out + n_side]
        for src, dst in zip(side_in, side_out):
            dst[...] = src[...].astype(dst.dtype)
        body(*refs[:n_in], *refs[o0:o0 + n_out], *refs[o0 + n_out + n_side:])

    return wrapped


def _call(body, grid, in_specs, out_specs, out_shapes, args, sides, semantics, name,
          scratch_shapes=()):
    n_in, n_out = len(in_specs), len(out_specs)
    outs = pl.pallas_call(
        _with_sides(body, n_in, n_out, len(sides)),
        grid=grid,
        in_specs=list(in_specs) + [s.in_spec for s in sides],
        out_specs=list(out_specs) + [s.out_spec for s in sides],
        out_shape=list(out_shapes) + [s.out_shape for s in sides],
        scratch_shapes=list(scratch_shapes),
        compiler_params=_params(*semantics),
        name=name,
    )(*args, *[s.w for s in sides])
    return outs[:n_out], outs[n_out:]


def _mm_kernel(*refs, has_res, d_scale, has_gain):
    refs = list(refs)
    a_ref, w_ref = refs.pop(0), refs.pop(0)
    r_ref = refs.pop(0) if has_res else None
    s_ref = refs.pop(0) if d_scale else None
    g_ref = refs.pop(0) if has_gain else None
    o_ref = refs.pop(0)
    if has_gain:
        hg_ref, ssq_ref = refs
    for rows in _row_chunks(a_ref.shape[0]):
        acc = jnp.dot(a_ref[rows, :], w_ref[...], preferred_element_type=F32)
        if d_scale:
            acc = acc * _inv_rms(s_ref, rows, d_scale)
        if has_res:
            acc = r_ref[rows, :] + acc
        o_ref[rows, :] = acc.astype(o_ref.dtype)
        if has_gain:
            _emit_scaled(acc, rows, g_ref, hg_ref, ssq_ref)


def matmul(a, w, out_dtype, tiles, *, name, residual=None, col_start=0, n=None,
           in_ssq=None, out_gain=None, converts=(), weight_major=False):
    m, k = a.shape
    n = w.shape[1] if n is None else n
    tm, tn = _tile(m, tiles[0]), _tile(n, tiles[1])
    assert col_start % tn == 0
    j0 = col_start // tn
    gi, gj = m // tm, n // tn
    if weight_major:
        grid = (gj, gi)
        ij = lambda f: (lambda j, i: f(i, j))
        w_mode = dict(pipeline_mode=pl.Buffered(1))
        step = lambda j, i: j * gi + i
    else:
        grid = (gi, gj)
        ij = lambda f: f
        w_mode = {}
        step = lambda i, j: i * gj + j
    in_specs = [pl.BlockSpec((tm, k), ij(lambda i, j: (i, 0))),
                pl.BlockSpec((k, tn), ij(lambda i, j: (0, j0 + j)), **w_mode)]
    args = [a, w]
    if residual is not None:
        in_specs.append(pl.BlockSpec((tm, tn), ij(lambda i, j: (i, j))))
        args.append(residual)
    if in_ssq is not None:
        in_specs.append(pl.BlockSpec((tm, in_ssq.shape[1]), ij(lambda i, j: (i, 0))))
        args.append(in_ssq)
    out_specs = [pl.BlockSpec((tm, tn), ij(lambda i, j: (i, j)))]
    out_shapes = [jax.ShapeDtypeStruct((m, n), out_dtype)]
    if out_gain is not None:
        in_specs.append(pl.BlockSpec((1, tn), ij(lambda i, j: (0, j))))
        args.append(out_gain.reshape(1, n))
        out_specs += [pl.BlockSpec((tm, tn), ij(lambda i, j: (i, j))),
                      pl.BlockSpec((tm, LANES), ij(lambda i, j: (i, j)))]
        out_shapes += [jax.ShapeDtypeStruct((m, n), BF16),
                       jax.ShapeDtypeStruct((m, gj * LANES), F32)]
    body = functools.partial(_mm_kernel, has_res=residual is not None,
                             d_scale=k if in_ssq is not None else 0,
                             has_gain=out_gain is not None)
    sides = [_Side(c, gi * gj, step) for c in converts]
    return _call(body, grid, in_specs, out_specs, out_shapes, args, sides,
                 ("arbitrary", "arbitrary"), name)


def _glu_kernel(a_ref, wg_ref, wu_ref, s_ref, o_ref, *, d_scale):
    for rows in _row_chunks(a_ref.shape[0]):
        a = a_ref[rows, :]
        inv = _inv_rms(s_ref, rows, d_scale)
        g = jnp.dot(a, wg_ref[...], preferred_element_type=F32) * inv
        u = jnp.dot(a, wu_ref[...], preferred_element_type=F32) * inv
        o_ref[rows, :] = (g * jax.nn.sigmoid(g) * u).astype(o_ref.dtype)


def glu_matmul(a, wg, wu, in_ssq, converts=()):
    m, k = a.shape
    n = wg.shape[1]
    tm, tn = _tile(m, GLU_TILES[0]), _tile(n, GLU_TILES[1])
    gj = n // tn
    sides = [_Side(c, (m // tm) * gj, lambda i, j: i * gj + j) for c in converts]
    return _call(
        functools.partial(_glu_kernel, d_scale=k), (m // tm, gj),
        [pl.BlockSpec((tm, k), lambda i, j: (i, 0)),
         pl.BlockSpec((k, tn), lambda i, j: (0, j)),
         pl.BlockSpec((k, tn), lambda i, j: (0, j)),
         pl.BlockSpec((tm, in_ssq.shape[1]), lambda i, j: (i, 0))],
        [pl.BlockSpec((tm, tn), lambda i, j: (i, j))],
        [jax.ShapeDtypeStruct((m, n), BF16)],
        [a, wg, wu, in_ssq], sides, ("parallel", "arbitrary"), "ffn_gate_up")


def _ple_kernel(a_ref, w_ref, s_ref, h_ref, p_ref, wp_ref, gp_ref, gf_ref, o_ref, pe_ref, ssq_ref, *,
                tn, d_scale, final_norm):
    j = pl.program_id(1)
    tm, n = o_ref.shape

    @pl.when(j == 0)
    def _():
        pe_ref[...] = _rms(jnp.dot(p_ref[...], wp_ref[...], preferred_element_type=F32),
                           gp_ref[...])
        ssq_ref[...] = jnp.zeros_like(ssq_ref)

    cols = pl.ds(pl.multiple_of(j * tn, tn), tn)
    for rows in _row_chunks(tm):
        z = (jnp.dot(a_ref[rows, :], w_ref[...], preferred_element_type=F32)
             * _inv_rms(s_ref, rows, d_scale))
        out = h_ref[rows, :] + jax.nn.sigmoid(z) * pe_ref[rows, cols]
        o_ref[rows, cols] = out
        if final_norm:
            part = jnp.sum(out * out, axis=1, keepdims=True)
            ssq_ref[rows, :] += jnp.broadcast_to(part, (out.shape[0], LANES))

    if final_norm:
        @pl.when(j == pl.num_programs(1) - 1)
        def _():
            step = 2 * SUBLANES
            for r0 in range(0, tm, step):
                rows = pl.ds(r0, step)
                o_ref[rows, :] = o_ref[rows, :] * _inv_rms(ssq_ref, rows, n) * gf_ref[...]


def gated_embedding_add(a, w, in_ssq, h, p, wp, gp, gf, final_norm):
    m, k = a.shape
    n = w.shape[1]
    kp = p.shape[1]
    tm, tn = _tile(m, PLE_TILES[0]), _tile(n, PLE_TILES[1])
    kern = functools.partial(_ple_kernel, tn=tn, d_scale=k, final_norm=final_norm)
    return pl.pallas_call(
        kern,
        grid=(m // tm, n // tn),
        in_specs=[pl.BlockSpec((tm, k), lambda i, j: (i, 0)),
                  pl.BlockSpec((k, tn), lambda i, j: (0, j)),
                  pl.BlockSpec((tm, in_ssq.shape[1]), lambda i, j: (i, 0)),
                  pl.BlockSpec((tm, tn), lambda i, j: (i, j)),
                  pl.BlockSpec((tm, kp), lambda i, j: (i, 0)),
                  pl.BlockSpec((kp, n), lambda i, j: (0, 0)),
                  pl.BlockSpec((1, n), lambda i, j: (0, 0)),
                  pl.BlockSpec((1, n), lambda i, j: (0, 0))],
        out_specs=pl.BlockSpec((tm, n), lambda i, j: (i, 0)),
        out_shape=jax.ShapeDtypeStruct((m, n), F32),
        scratch_shapes=[pltpu.VMEM((tm, n), F32), pltpu.VMEM((tm, LANES), F32)],
        compiler_params=_params("parallel", "arbitrary"),
        name="ple_gate",
    )(a, w, in_ssq, h, p, wp, gp.reshape(1, n), gf.reshape(1, n))


def _suffix_sum_matrix(n):
    r = lax.broadcasted_iota(jnp.int32, (2 * n, n), 0) % n
    c = lax.broadcasted_iota(jnp.int32, (2 * n, n), 1)
    return jnp.where(r >= c, 1.0, 0.0).astype(BF16)


def _attn_kernel(q_ref, k_ref, v_ref, t_ref, o_ref, acc_ref, carry_ref, *, tq, heads, scale):
    qi = pl.program_id(2)
    groups = tq // LANES
    sign_bit = jnp.uint32(0x80000000)
    split = tq * ATTN_SPLIT_16THS // 16 // (2 * SUBLANES) * (2 * SUBLANES)
    assert 0 < split < tq

    def all_heads(kstart, masked, nrows=tq):
        hs = range(heads)
        cols = [slice(h * D_HEAD, (h + 1) * D_HEAD) for h in hs]
        if masked:
            row = lax.broadcasted_iota(jnp.int32, (tq, tq), 0)
            col = lax.broadcasted_iota(jnp.int32, (tq, tq), 1)
            causal = col < row
        nz = [lax.dot_general(q_ref[0:nrows, cols[h]], k_ref[pl.ds(kstart, tq), cols[h]],
                              (((1,), (1,)), ((), ())), preferred_element_type=F32)
              * (-scale * LOG2_E) for h in hs]
        sums = []
        for h in hs:
            neg_abs = lax.bitcast_convert_type(
                lax.bitcast_convert_type(nz[h], jnp.uint32) | sign_bit, F32)
            log_keep = jnp.minimum(nz[h], 0.0) - jnp.log(1.0 + jnp.exp2(neg_abs)) * LOG2_E
            if masked:
                log_keep = jnp.where(causal, log_keep, 0.0)
            hi = log_keep.astype(BF16)
            lo = (log_keep - hi.astype(F32)).astype(BF16)
            sums.append(jnp.dot(jnp.concatenate([hi, lo], axis=1), t_ref[...],
                                preferred_element_type=F32))
        tops = [None, None]
        for h in hs:
            x = sums[h] - nz[h]
            total = jnp.broadcast_to(sums[h][:, 0:1], (nrows, LANES))
            vc = v_ref[pl.ds(kstart, tq), cols[h]]
            if masked:
                w = jnp.where(causal, jnp.exp2(x), 0.0)
                acc_ref[h] = jnp.dot(w.astype(BF16), vc, preferred_element_type=F32)
                carry = total
            else:
                carry = carry_ref[h, 0:nrows]
                w = jnp.exp2(x + jnp.concatenate([carry] * groups, axis=1))
                acc_ref[h, 0:nrows] += jnp.dot(w.astype(BF16), vc, preferred_element_type=F32)
                carry = carry + total
            carry_ref[h, 0:nrows] = carry
            for part, rows in enumerate((slice(0, split), slice(split, nrows))):
                if rows.start < rows.stop:
                    m = jnp.max(carry[rows])
                    tops[part] = m if tops[part] is None else jnp.maximum(tops[part], m)
        return tops

    _, top_hi = all_heads(pl.multiple_of(qi * tq, tq), True)
    top_lo = jnp.float32(0.0)

    def more(c):
        j, top_lo, top_hi = c
        return jnp.logical_and(j < qi, jnp.maximum(top_lo, top_hi) >= MIN_LOG2_WEIGHT)

    def body(c):
        j, _, top_hi = c
        kstart = pl.multiple_of((qi - 1 - j) * tq, tq)
        top_lo, top_hi = lax.cond(
            top_hi >= MIN_LOG2_WEIGHT,
            lambda: tuple(all_heads(kstart, False)),
            lambda: (all_heads(kstart, False, nrows=split)[0], top_hi))
        return j + 1, top_lo, top_hi

    lax.while_loop(more, body, (jnp.int32(0), top_lo, top_hi))

    for h in range(heads):
        o_ref[:, h * D_HEAD:(h + 1) * D_HEAD] = acc_ref[h]


def stick_breaking_attention(qkv, batch, seq, d_attn, converts=()):
    n_heads = d_attn // D_HEAD
    heads = min(ATTN_HEADS, n_heads)
    assert n_heads % heads == 0
    width = heads * D_HEAD
    pairs = d_attn // width
    tq = _tile(seq, ATTN_ROWS)
    nq = seq // tq
    kern = functools.partial(_attn_kernel, tq=tq, heads=heads, scale=D_HEAD ** -0.5)
    sides = [_Side(c, batch * pairs * nq, lambda b, p, i: (b * pairs + p) * nq + i)
             for c in converts]
    return _call(
        kern, (batch, pairs, nq),
        [pl.BlockSpec((tq, width), lambda b, p, i: (b * nq + i, p)),
         pl.BlockSpec((seq, width), lambda b, p, i: (b, pairs + p)),
         pl.BlockSpec((seq, width), lambda b, p, i: (b, 2 * pairs + p)),
         pl.BlockSpec((2 * tq, tq), lambda b, p, i: (0, 0))],
        [pl.BlockSpec((tq, width), lambda b, p, i: (b * nq + i, p))],
        [jax.ShapeDtypeStruct((batch * seq, d_attn), F32)],
        [qkv, qkv, qkv, _suffix_sum_matrix(tq)], sides,
        ("parallel", "parallel", "arbitrary"), "stick_breaking_attention",
        scratch_shapes=[pltpu.VMEM((heads, tq, D_HEAD), F32),
                        pltpu.VMEM((heads, tq, LANES), F32)])


def _softplus(x):
    return jnp.maximum(x, 0.0) + jnp.log1p(jnp.exp(-jnp.abs(x)))


def _mixer_kernel(xr_ref, gr_ref, attn_ref, cw_ref, cb_ref, wg_ref, ba_ref, bx_ref, lam_ref,
                  ga_ref, gn_ref, o_ref, tail_ref, h_ref, rec_ref, *, ts, n_blocks, conv_width):
    @pl.when(pl.program_id(1) == 0)
    def _():
        tail_ref[...] = jnp.zeros_like(tail_ref)
        h_ref[...] = jnp.zeros_like(h_ref)

    row8 = lax.broadcasted_iota(jnp.int32, (SUBLANES, LANES), 0)
    sub3 = lax.broadcasted_iota(jnp.int32, (ts // SUBLANES, SUBLANES, LANES), 1)

    def block(n, ssq):
        lanes = pl.ds(pl.multiple_of(n * LANES, LANES), LANES)
        x = xr_ref[:, lanes]
        tail = tail_ref[:, lanes]
        assert conv_width - 1 < SUBLANES
        y = cb_ref[:, lanes] + cw_ref[conv_width - 1:conv_width, lanes] * x
        x3 = x.reshape(ts // SUBLANES, SUBLANES, LANES)
        for d in range(1, conv_width):
            rot = pltpu.roll(x3, d, axis=1)
            prev = jnp.concatenate([pltpu.roll(tail, d, axis=0)[None], rot[:-1]], axis=0)
            xs = jnp.where(sub3 >= d, rot, prev).reshape(ts, LANES)
            y = y + cw_ref[conv_width - 1 - d:conv_width - d, lanes] * xs
        tail_ref[:, lanes] = x[ts - SUBLANES:]

        gates = jnp.dot(y.astype(BF16), wg_ref[n], preferred_element_type=F32)
        r = jax.nn.sigmoid(gates[:, :LANES] + ba_ref[:, lanes])
        i = jax.nn.sigmoid(gates[:, LANES:] + bx_ref[:, lanes])
        log_a = -RGLRU_C * r * _softplus(-lam_ref[:, lanes])
        a = jnp.exp(log_a)
        v = -jnp.tanh(log_a) * (a * a + 1.0)
        u = v * lax.rsqrt(jnp.maximum(v, F32_TINY)) * (i * y)

        a = a.reshape(ts // SUBLANES, SUBLANES, LANES)
        u = u.reshape(ts // SUBLANES, SUBLANES, LANES)
        for d in (1, 2, 4):
            keep = sub3 >= d
            a_prev = jnp.where(keep, pltpu.roll(a, d, axis=1), 1.0)
            u_prev = jnp.where(keep, pltpu.roll(u, d, axis=1), 0.0)
            u = a * u_prev + u
            a = a * a_prev
        a = a.reshape(ts, LANES)
        u = u.reshape(ts, LANES)
        h_prev = h_ref[:, lanes]
        hs = []
        for g in range(ts // SUBLANES):
            rows = slice(g * SUBLANES, (g + 1) * SUBLANES)
            hg = a[rows] * h_prev + u[rows]
            hs.append(hg)
            h_prev = jnp.broadcast_to(hg[SUBLANES - 1:SUBLANES], (SUBLANES, LANES))
        h_ref[:, lanes] = h_prev
        rec = jnp.concatenate(hs, axis=0) * jax.nn.gelu(gr_ref[:, lanes], approximate=True)
        rec_ref[:, lanes] = rec
        return ssq + jnp.sum(rec * rec, axis=1, keepdims=True)

    ssq = lax.fori_loop(0, n_blocks, block, jnp.zeros((ts, 1), F32),
                        unroll=2 if n_blocks % 2 == 0 else 1)
    d_attn = attn_ref.shape[1]
    d_rnn = n_blocks * LANES
    inv = lax.rsqrt(ssq / d_rnn + NORM_EPS)
    step = 2 * SUBLANES
    for r0 in range(0, ts, step):
        rows = pl.ds(r0, step)
        o_ref[rows, :d_attn] = _rms(attn_ref[rows, :], ga_ref[...]).astype(o_ref.dtype)
        o_ref[rows, d_attn:] = (rec_ref[rows, :] * inv[r0:r0 + step]
                                * gn_ref[...]).astype(o_ref.dtype)


def mixer_epilogue(xg, attn, conv_w, conv_b, w_gates, b_a, b_x, lam, g_attn, g_rnn, batch, seq):
    d_rnn = xg.shape[1] // 2
    d_attn = attn.shape[1]
    n_blocks = d_rnn // LANES
    assert w_gates.shape == (n_blocks, LANES, 2 * LANES)
    conv_width = conv_w.shape[0]
    ts = _tile(seq, SCAN_ROWS)
    ns = seq // ts
    row1 = lambda v: v.reshape(1, -1)
    const = lambda shape: pl.BlockSpec(shape, lambda b, s: (0,) * len(shape))
    kern = functools.partial(_mixer_kernel, ts=ts, n_blocks=n_blocks, conv_width=conv_width)
    return pl.pallas_call(
        kern,
        grid=(batch, ns),
        in_specs=[pl.BlockSpec((ts, d_rnn), lambda b, s: (b * ns + s, 0)),
                  pl.BlockSpec((ts, d_rnn), lambda b, s: (b * ns + s, 1)),
                  pl.BlockSpec((ts, d_attn), lambda b, s: (b * ns + s, 0)),
                  const((conv_width, d_rnn)), const((1, d_rnn)),
                  const((n_blocks, LANES, 2 * LANES)),
                  const((1, d_rnn)), const((1, d_rnn)), const((1, d_rnn)),
                  const((1, d_attn)), const((1, d_rnn))],
        out_specs=pl.BlockSpec((ts, d_attn + d_rnn), lambda b, s: (b * ns + s, 0)),
        out_shape=jax.ShapeDtypeStruct((batch * seq, d_attn + d_rnn), BF16),
        scratch_shapes=[pltpu.VMEM((SUBLANES, d_rnn), F32),
                        pltpu.VMEM((SUBLANES, d_rnn), F32),
                        pltpu.VMEM((ts, d_rnn), F32)],
        compiler_params=_params("parallel", "arbitrary"),
        name="rglru_mixer",
    )(xg, xg, attn, conv_w, row1(conv_b), w_gates, row1(b_a), row1(b_x), row1(lam),
      row1(g_attn), row1(g_rnn))


def kernel(x, p, g_mix, w_in, conv_w, conv_b, w_rg_a, b_rg_a, w_rg_x, b_rg_x, rg_lambda,
           g_attn_out, g_rnn_out, w_out, g_ffn, w_ffn_gate, w_ffn_up, w_ffn_down, g_ple,
           w_ple_gate, w_ple_proj, g_ple_out, g_final):
    batch, seq, d_model = x.shape
    depth = w_in.shape[0]
    d_attn = g_attn_out.shape[-1]
    d_rnn = g_rnn_out.shape[-1]
    m = batch * seq

    h = x.reshape(m, d_model)
    for l in range(depth):
        u = rmsnorm(h, g_mix[l], BF16)
        w_in_l = w_in[l].astype(BF16)
        (qkv,), (w_out_l,) = matmul(u, w_in_l, BF16, PROJ_TILES, name="in_proj_qkv",
                                    n=3 * d_attn, converts=[w_out[l]])
        (xg,), (w_ple_l,) = matmul(u, w_in_l, F32, PROJ_TILES, name="in_proj_rnn",
                                   col_start=3 * d_attn, n=2 * d_rnn, converts=[w_ple_gate[l]])
        (attn,), (w_gate_l, w_up_l) = stick_breaking_attention(
            qkv, batch, seq, d_attn, converts=[w_ffn_gate[l], w_ffn_up[l]])
        w_gates = jnp.concatenate([w_rg_a[l], w_rg_x[l]], axis=-1).astype(BF16)
        mixed = mixer_epilogue(xg, attn, conv_w[l], conv_b[l], w_gates, b_rg_a[l], b_rg_x[l],
                               rg_lambda[l], g_attn_out[l], g_rnn_out[l], batch, seq)
        (h, hg, ssq), _ = matmul(mixed, w_out_l, F32, OUT_TILES, residual=h,
                                 out_gain=g_ffn[l], name="out_proj")

        (hidden,), (w_down_l,) = glu_matmul(hg, w_gate_l, w_up_l, ssq, converts=[w_ffn_down[l]])
        (h, hg, ssq), _ = matmul(hidden, w_down_l, F32, DOWN_TILES, residual=h,
                                 out_gain=g_ple[l], name="ffn_down", weight_major=True)

        h = gated_embedding_add(hg, w_ple_l, ssq, h,
                                p[l].reshape(m, -1).astype(BF16), w_ple_proj[l].astype(BF16),
                                g_ple_out[l], g_final, final_norm=(l == depth - 1))
    return h.reshape(batch, seq, d_model)
```

```python
import functools

import jax
import jax.numpy as jnp
from jax import lax
from jax.experimental import pallas as pl
from jax.experimental.pallas import tpu as pltpu

F32 = jnp.float32
BF16 = jnp.bfloat16

NORM_EPS = 1e-6
D_HEAD = 128
RGLRU_C = 8.0
LOG2_E = 1.4426950408889634
MIN_LOG2_WEIGHT = -150.0
F32_TINY = 1.1754943508222875e-38

LANES = 128
SUBLANES = 8
VMEM_LIMIT_BYTES = 60 * 1024 * 1024

PROJ_TILES = (1024, 1024)
OUT_TILES = (1024, 1024)
GLU_TILES = (2048, 256)
DOWN_TILES = (512, 1024)
PLE_TILES = (512, 512)
NORM_ROWS = 256
ATTN_ROWS = 256
ATTN_HEADS = 8
ATTN_SPLIT_16THS = 11
SCAN_ROWS = 256
EPILOGUE_ROWS = 256


def _params(*semantics):
    return pltpu.CompilerParams(dimension_semantics=semantics,
                                vmem_limit_bytes=VMEM_LIMIT_BYTES)


def _tile(n, want):
    t = min(n, want)
    assert n % t == 0, (n, want)
    return t


def _rms(x, g):
    y = x * lax.rsqrt(jnp.mean(x * x, axis=-1, keepdims=True) + NORM_EPS)
    return y * g


def _rmsnorm_kernel(x_ref, g_ref, o_ref):
    o_ref[...] = _rms(x_ref[...], g_ref[...]).astype(o_ref.dtype)


def rmsnorm(x, g, out_dtype):
    m, d = x.shape
    tm = _tile(m, NORM_ROWS)
    return pl.pallas_call(
        _rmsnorm_kernel,
        grid=(m // tm,),
        in_specs=[pl.BlockSpec((tm, d), lambda i: (i, 0)),
                  pl.BlockSpec((1, d), lambda i: (0, 0))],
        out_specs=pl.BlockSpec((tm, d), lambda i: (i, 0)),
        out_shape=jax.ShapeDtypeStruct((m, d), out_dtype),
        compiler_params=_params("parallel"),
        name="rmsnorm",
    )(x, g.reshape(1, d))


def _row_chunks(rows):
    ch = min(rows, EPILOGUE_ROWS)
    assert rows % ch == 0
    return [pl.ds(c * ch, ch) for c in range(rows // ch)]


def _inv_rms(ssq_ref, rows, d):
    parts = ssq_ref.shape[1] // LANES
    total = ssq_ref[rows, 0:1]
    for p in range(1, parts):
        total = total + ssq_ref[rows, p * LANES:p * LANES + 1]
    return lax.rsqrt(total * (1.0 / d) + NORM_EPS)


def _emit_scaled(h, rows, g_ref, hg_ref, ssq_ref):
    hg_ref[rows, :] = (h * g_ref[...]).astype(hg_ref.dtype)
    part = jnp.sum(h * h, axis=1, keepdims=True)
    ssq_ref[rows, :] = jnp.broadcast_to(part, (h.shape[0], ssq_ref.shape[1]))


class _Side:
    def __init__(self, w, n_steps, linear_step):
        rows, cols = w.shape
        units = rows // (2 * SUBLANES)
        assert units * 2 * SUBLANES == rows
        n_chunks = max(c for c in range(1, min(units, n_steps) + 1) if units % c == 0)
        chunk = rows // n_chunks
        index = lambda *g: (jnp.minimum(linear_step(*g), n_chunks - 1), 0)
        self.w = w
        self.in_spec = pl.BlockSpec((chunk, cols), index)
        self.out_spec = pl.BlockSpec((chunk, cols), index)
        self.out_shape = jax.ShapeDtypeStruct((rows, cols), BF16)


def _with_sides(body, n_in, n_out, n_side):
    if n_side == 0:
        return body

    def wrapped(*refs):
        side_in = refs[n_in:n_in + n_side]
        o0 = n_in + n_side
        side_out = refs[o0 + n_out:o0 + n_out + n_side]
        for src, dst in zip(side_in, side_out):
            dst[...] = src[...].astype(dst.dtype)
        body(*refs[:n_in], *refs[o0:o0 + n_out], *refs[o0 + n_out + n_side:])

    return wrapped


def _call(body, grid, in_specs, out_specs, out_shapes, args, sides, semantics, name,
          scratch_shapes=()):
    n_in, n_out = len(in_specs), len(out_specs)
    outs = pl.pallas_call(
        _with_sides(body, n_in, n_out, len(sides)),
        grid=grid,
        in_specs=list(in_specs) + [s.in_spec for s in sides],
        out_specs=list(out_specs) + [s.out_spec for s in sides],
        out_shape=list(out_shapes) + [s.out_shape for s in sides],
        scratch_shapes=list(scratch_shapes),
        compiler_params=_params(*semantics),
        name=name,
    )(*args, *[s.w for s in sides])
    return outs[:n_out], outs[n_out:]


def _mm_kernel(*refs, has_res, d_scale, has_gain):
    refs = list(refs)
    a_ref, w_ref = refs.pop(0), refs.pop(0)
    r_ref = refs.pop(0) if has_res else None
    s_ref = refs.pop(0) if d_scale else None
    g_ref = refs.pop(0) if has_gain else None
    o_ref = refs.pop(0)
    if has_gain:
        hg_ref, ssq_ref = refs
    for rows in _row_chunks(a_ref.shape[0]):
        acc = jnp.dot(a_ref[rows, :], w_ref[...], preferred_element_type=F32)
        if d_scale:
            acc = acc * _inv_rms(s_ref, rows, d_scale)
        if has_res:
            acc = r_ref[rows, :] + acc
        o_ref[rows, :] = acc.astype(o_ref.dtype)
        if has_gain:
            _emit_scaled(acc, rows, g_ref, hg_ref, ssq_ref)


def matmul(a, w, out_dtype, tiles, *, name, residual=None, col_start=0, n=None,
           in_ssq=None, out_gain=None, converts=(), weight_major=False):
    m, k = a.shape
    n = w.shape[1] if n is None else n
    tm, tn = _tile(m, tiles[0]), _tile(n, tiles[1])
    assert col_start % tn == 0
    j0 = col_start // tn
    gi, gj = m // tm, n // tn
    if weight_major:
        grid = (gj, gi)
        ij = lambda f: (lambda j, i: f(i, j))
        w_mode = dict(pipeline_mode=pl.Buffered(1))
        step = lambda j, i: j * gi + i
    else:
        grid = (gi, gj)
        ij = lambda f: f
        w_mode = {}
        step = lambda i, j: i * gj + j
    in_specs = [pl.BlockSpec((tm, k), ij(lambda i, j: (i, 0))),
                pl.BlockSpec((k, tn), ij(lambda i, j: (0, j0 + j)), **w_mode)]
    args = [a, w]
    if residual is not None:
        in_specs.append(pl.BlockSpec((tm, tn), ij(lambda i, j: (i, j))))
        args.append(residual)
    if in_ssq is not None:
        in_specs.append(pl.BlockSpec((tm, in_ssq.shape[1]), ij(lambda i, j: (i, 0))))
        args.append(in_ssq)
    out_specs = [pl.BlockSpec((tm, tn), ij(lambda i, j: (i, j)))]
    out_shapes = [jax.ShapeDtypeStruct((m, n), out_dtype)]
    if out_gain is not None:
        in_specs.append(pl.BlockSpec((1, tn), ij(lambda i, j: (0, j))))
        args.append(out_gain.reshape(1, n))
        out_specs += [pl.BlockSpec((tm, tn), ij(lambda i, j: (i, j))),
                      pl.BlockSpec((tm, LANES), ij(lambda i, j: (i, j)))]
        out_shapes += [jax.ShapeDtypeStruct((m, n), BF16),
                       jax.ShapeDtypeStruct((m, gj * LANES), F32)]
    body = functools.partial(_mm_kernel, has_res=residual is not None,
                             d_scale=k if in_ssq is not None else 0,
                             has_gain=out_gain is not None)
    sides = [_Side(c, gi * gj, step) for c in converts]
    return _call(body, grid, in_specs, out_specs, out_shapes, args, sides,
                 ("arbitrary", "arbitrary"), name)


def _glu_kernel(a_ref, wg_ref, wu_ref, s_ref, o_ref, *, d_scale):
    for rows in _row_chunks(a_ref.shape[0]):
        a = a_ref[rows, :]
        inv = _inv_rms(s_ref, rows, d_scale)
        g = jnp.dot(a, wg_ref[...], preferred_element_type=F32) * inv
        u = jnp.dot(a, wu_ref[...], preferred_element_type=F32) * inv
        o_ref[rows, :] = (g * jax.nn.sigmoid(g) * u).astype(o_ref.dtype)


def glu_matmul(a, wg, wu, in_ssq, converts=()):
    m, k = a.shape
    n = wg.shape[1]
    tm, tn = _tile(m, GLU_TILES[0]), _tile(n, GLU_TILES[1])
    gj = n // tn
    sides = [_Side(c, (m // tm) * gj, lambda i, j: i * gj + j) for c in converts]
    return _call(
        functools.partial(_glu_kernel, d_scale=k), (m // tm, gj),
        [pl.BlockSpec((tm, k), lambda i, j: (i, 0)),
         pl.BlockSpec((k, tn), lambda i, j: (0, j)),
         pl.BlockSpec((k, tn), lambda i, j: (0, j)),
         pl.BlockSpec((tm, in_ssq.shape[1]), lambda i, j: (i, 0))],
        [pl.BlockSpec((tm, tn), lambda i, j: (i, j))],
        [jax.ShapeDtypeStruct((m, n), BF16)],
        [a, wg, wu, in_ssq], sides, ("parallel", "arbitrary"), "ffn_gate_up")


def _ple_kernel(a_ref, w_ref, s_ref, h_ref, p_ref, wp_ref, gp_ref, gf_ref, o_ref, pe_ref, ssq_ref, *,
                tn, d_scale, final_norm):
    j = pl.program_id(1)
    tm, n = o_ref.shape

    @pl.when(j == 0)
    def _():
        pe_ref[...] = _rms(jnp.dot(p_ref[...], wp_ref[...], preferred_element_type=F32),
                           gp_ref[...])
        ssq_ref[...] = jnp.zeros_like(ssq_ref)

    cols = pl.ds(pl.multiple_of(j * tn, tn), tn)
    for rows in _row_chunks(tm):
        z = (jnp.dot(a_ref[rows, :], w_ref[...], preferred_element_type=F32)
             * _inv_rms(s_ref, rows, d_scale))
        out = h_ref[rows, :] + jax.nn.sigmoid(z) * pe_ref[rows, cols]
        o_ref[rows, cols] = out
        if final_norm:
            part = jnp.sum(out * out, axis=1, keepdims=True)
            ssq_ref[rows, :] += jnp.broadcast_to(part, (out.shape[0], LANES))

    if final_norm:
        @pl.when(j == pl.num_programs(1) - 1)
        def _():
            step = 2 * SUBLANES
            for r0 in range(0, tm, step):
                rows = pl.ds(r0, step)
                o_ref[rows, :] = o_ref[rows, :] * _inv_rms(ssq_ref, rows, n) * gf_ref[...]


def gated_embedding_add(a, w, in_ssq, h, p, wp, gp, gf, final_norm):
    m, k = a.shape
    n = w.shape[1]
    kp = p.shape[1]
    tm, tn = _tile(m, PLE_TILES[0]), _tile(n, PLE_TILES[1])
    kern = functools.partial(_ple_kernel, tn=tn, d_scale=k, final_norm=final_norm)
    return pl.pallas_call(
        kern,
        grid=(m // tm, n // tn),
        in_specs=[pl.BlockSpec((tm, k), lambda i, j: (i, 0)),
                  pl.BlockSpec((k, tn), lambda i, j: (0, j)),
                  pl.BlockSpec((tm, in_ssq.shape[1]), lambda i, j: (i, 0)),
                  pl.BlockSpec((tm, tn), lambda i, j: (i, j)),
                  pl.BlockSpec((tm, kp), lambda i, j: (i, 0)),
                  pl.BlockSpec((kp, n), lambda i, j: (0, 0)),
                  pl.BlockSpec((1, n), lambda i, j: (0, 0)),
                  pl.BlockSpec((1, n), lambda i, j: (0, 0))],
        out_specs=pl.BlockSpec((tm, n), lambda i, j: (i, 0)),
        out_shape=jax.ShapeDtypeStruct((m, n), F32),
        scratch_shapes=[pltpu.VMEM((tm, n), F32), pltpu.VMEM((tm, LANES), F32)],
        compiler_params=_params("parallel", "arbitrary"),
        name="ple_gate",
    )(a, w, in_ssq, h, p, wp, gp.reshape(1, n), gf.reshape(1, n))


def _suffix_sum_matrix(n):
    r = lax.broadcasted_iota(jnp.int32, (2 * n, n), 0) % n
    c = lax.broadcasted_iota(jnp.int32, (2 * n, n), 1)
    return jnp.where(r >= c, 1.0, 0.0).astype(BF16)


def _attn_kernel(q_ref, k_ref, v_ref, t_ref, o_ref, acc_ref, carry_ref, *, tq, heads, scale):
    qi = pl.program_id(2)
    sign_bit = jnp.uint32(0x80000000)
    split = tq * ATTN_SPLIT_16THS // 16 // (2 * SUBLANES) * (2 * SUBLANES)
    assert 0 < split < tq

    def all_heads(kstart, masked, r0=0, r1=tq, nkeys=tq):
        hs = range(heads)
        nrows = r1 - r0
        cols = [slice(h * D_HEAD, (h + 1) * D_HEAD) for h in hs]
        keys = pl.ds(kstart, nkeys)
        if masked:
            row = r0 + lax.broadcasted_iota(jnp.int32, (nrows, nkeys), 0)
            col = lax.broadcasted_iota(jnp.int32, (nrows, nkeys), 1)
            causal = col < row
        if nkeys == tq:
            tri = t_ref[...]
        else:
            tri = jnp.concatenate([t_ref[0:nkeys, 0:nkeys]] * 2, axis=0)
        nz = [lax.dot_general(q_ref[r0:r1, cols[h]], k_ref[keys, cols[h]],
                              (((1,), (1,)), ((), ())), preferred_element_type=F32)
              * (-scale * LOG2_E) for h in hs]
        sums = []
        for h in hs:
            neg_abs = lax.bitcast_convert_type(
                lax.bitcast_convert_type(nz[h], jnp.uint32) | sign_bit, F32)
            log_keep = jnp.minimum(nz[h], 0.0) - jnp.log(1.0 + jnp.exp2(neg_abs)) * LOG2_E
            if masked:
                log_keep = jnp.where(causal, log_keep, 0.0)
            hi = log_keep.astype(BF16)
            lo = (log_keep - hi.astype(F32)).astype(BF16)
            sums.append(jnp.dot(jnp.concatenate([hi, lo], axis=1), tri,
                                preferred_element_type=F32))
        tops = [None, None]
        for h in hs:
            x = sums[h] - nz[h]
            total = jnp.broadcast_to(sums[h][:, 0:1], (nrows, LANES))
            vc = v_ref[keys, cols[h]]
            if masked:
                w = jnp.where(causal, jnp.exp2(x), 0.0)
                acc_ref[h, r0:r1] = jnp.dot(w.astype(BF16), vc, preferred_element_type=F32)
                carry = total
            else:
                carry = carry_ref[h, r0:r1]
                w = jnp.exp2(x + jnp.concatenate([carry] * (nkeys // LANES), axis=1))
                acc_ref[h, r0:r1] += jnp.dot(w.astype(BF16), vc, preferred_element_type=F32)
                carry = carry + total
            carry_ref[h, r0:r1] = carry
            for part, (a, b) in enumerate(((r0, min(r1, split)), (max(r0, split), r1))):
                if a < b:
                    m = jnp.max(carry[a - r0:b - r0])
                    tops[part] = m if tops[part] is None else jnp.maximum(tops[part], m)
        return tops

    q0 = pl.multiple_of(qi * tq, tq)
    half = tq // 2
    if half % LANES == 0 and half <= split:
        all_heads(q0, True, 0, half, half)
        _, top_hi = all_heads(q0, True, half, tq, tq)
    else:
        _, top_hi = all_heads(q0, True)
    top_lo = jnp.float32(0.0)

    def more(c):
        j, top_lo, top_hi = c
        return jnp.logical_and(j < qi, jnp.maximum(top_lo, top_hi) >= MIN_LOG2_WEIGHT)

    def body(c):
        j, _, top_hi = c
        kstart = pl.multiple_of((qi - 1 - j) * tq, tq)
        top_lo, top_hi = lax.cond(
            top_hi >= MIN_LOG2_WEIGHT,
            lambda: tuple(all_heads(kstart, False)),
            lambda: (all_heads(kstart, False, 0, split)[0], top_hi))
        return j + 1, top_lo, top_hi

    lax.while_loop(more, body, (jnp.int32(0), top_lo, top_hi))

    for h in range(heads):
        o_ref[:, h * D_HEAD:(h + 1) * D_HEAD] = acc_ref[h]


def stick_breaking_attention(qkv, batch, seq, d_attn, converts=()):
    n_heads = d_attn // D_HEAD
    heads = min(ATTN_HEADS, n_heads)
    assert n_heads % heads == 0
    width = heads * D_HEAD
    pairs = d_attn // width
    tq = _tile(seq, ATTN_ROWS)
    nq = seq // tq
    kern = functools.partial(_attn_kernel, tq=tq, heads=heads, scale=D_HEAD ** -0.5)
    sides = [_Side(c, batch * pairs * nq, lambda b, p, i: (b * pairs + p) * nq + i)
             for c in converts]
    return _call(
        kern, (batch, pairs, nq),
        [pl.BlockSpec((tq, width), lambda b, p, i: (b * nq + i, p)),
         pl.BlockSpec((seq, width), lambda b, p, i: (b, pairs + p)),
         pl.BlockSpec((seq, width), lambda b, p, i: (b, 2 * pairs + p)),
         pl.BlockSpec((2 * tq, tq), lambda b, p, i: (0, 0))],
        [pl.BlockSpec((tq, width), lambda b, p, i: (b * nq + i, p))],
        [jax.ShapeDtypeStruct((batch * seq, d_attn), F32)],
        [qkv, qkv, qkv, _suffix_sum_matrix(tq)], sides,
        ("parallel", "parallel", "arbitrary"), "stick_breaking_attention",
        scratch_shapes=[pltpu.VMEM((heads, tq, D_HEAD), F32),
                        pltpu.VMEM((heads, tq, LANES), F32)])


def _softplus(x):
    return jnp.maximum(x, 0.0) + jnp.log1p(jnp.exp(-jnp.abs(x)))


def _mixer_kernel(xr_ref, gr_ref, attn_ref, cw_ref, cb_ref, wg_ref, ba_ref, bx_ref, lam_ref,
                  ga_ref, gn_ref, o_ref, tail_ref, h_ref, rec_ref, *, ts, n_blocks, conv_width):
    @pl.when(pl.program_id(1) == 0)
    def _():
        tail_ref[...] = jnp.zeros_like(tail_ref)
        h_ref[...] = jnp.zeros_like(h_ref)

    row8 = lax.broadcasted_iota(jnp.int32, (SUBLANES, LANES), 0)
    sub3 = lax.broadcasted_iota(jnp.int32, (ts // SUBLANES, SUBLANES, LANES), 1)

    def block(n, ssq):
        lanes = pl.ds(pl.multiple_of(n * LANES, LANES), LANES)
        x = xr_ref[:, lanes]
        tail = tail_ref[:, lanes]
        assert conv_width - 1 < SUBLANES
        y = cb_ref[:, lanes] + cw_ref[conv_width - 1:conv_width, lanes] * x
        x3 = x.reshape(ts // SUBLANES, SUBLANES, LANES)
        for d in range(1, conv_width):
            rot = pltpu.roll(x3, d, axis=1)
            prev = jnp.concatenate([pltpu.roll(tail, d, axis=0)[None], rot[:-1]], axis=0)
            xs = jnp.where(sub3 >= d, rot, prev).reshape(ts, LANES)
            y = y + cw_ref[conv_width - 1 - d:conv_width - d, lanes] * xs
        tail_ref[:, lanes] = x[ts - SUBLANES:]

        gates = jnp.dot(y.astype(BF16), wg_ref[n], preferred_element_type=F32)
        r = jax.nn.sigmoid(gates[:, :LANES] + ba_ref[:, lanes])
        i = jax.nn.sigmoid(gates[:, LANES:] + bx_ref[:, lanes])
        log_a = -RGLRU_C * r * _softplus(-lam_ref[:, lanes])
        a = jnp.exp(log_a)
        v = -jnp.tanh(log_a) * (a * a + 1.0)
        u = v * lax.rsqrt(jnp.maximum(v, F32_TINY)) * (i * y)

        a = a.reshape(ts // SUBLANES, SUBLANES, LANES)
        u = u.reshape(ts // SUBLANES, SUBLANES, LANES)
        for d in (1, 2, 4):
            keep = sub3 >= d
            a_prev = jnp.where(keep, pltpu.roll(a, d, axis=1), 1.0)
            u_prev = jnp.where(keep, pltpu.roll(u, d, axis=1), 0.0)
            u = a * u_prev + u
            a = a * a_prev
        a = a.reshape(ts, LANES)
        u = u.reshape(ts, LANES)
        h_prev = h_ref[:, lanes]
        hs = []
        for g in range(ts // SUBLANES):
            rows = slice(g * SUBLANES, (g + 1) * SUBLANES)
            hg = a[rows] * h_prev + u[rows]
            hs.append(hg)
            h_prev = jnp.broadcast_to(hg[SUBLANES - 1:SUBLANES], (SUBLANES, LANES))
        h_ref[:, lanes] = h_prev
        rec = jnp.concatenate(hs, axis=0) * jax.nn.gelu(gr_ref[:, lanes], approximate=True)
        rec_ref[:, lanes] = rec
        return ssq + jnp.sum(rec * rec, axis=1, keepdims=True)

    ssq = lax.fori_loop(0, n_blocks, block, jnp.zeros((ts, 1), F32),
                        unroll=2 if n_blocks % 2 == 0 else 1)
    d_attn = attn_ref.shape[1]
    d_rnn = n_blocks * LANES
    inv = lax.rsqrt(ssq / d_rnn + NORM_EPS)
    step = 2 * SUBLANES
    for r0 in range(0, ts, step):
        rows = pl.ds(r0, step)
        o_ref[rows, :d_attn] = _rms(attn_ref[rows, :], ga_ref[...]).astype(o_ref.dtype)
        o_ref[rows, d_attn:] = (rec_ref[rows, :] * inv[r0:r0 + step]
                                * gn_ref[...]).astype(o_ref.dtype)


def mixer_epilogue(xg, attn, conv_w, conv_b, w_gates, b_a, b_x, lam, g_attn, g_rnn, batch, seq):
    d_rnn = xg.shape[1] // 2
    d_attn = attn.shape[1]
    n_blocks = d_rnn // LANES
    assert w_gates.shape == (n_blocks, LANES, 2 * LANES)
    conv_width = conv_w.shape[0]
    ts = _tile(seq, SCAN_ROWS)
    ns = seq // ts
    row1 = lambda v: v.reshape(1, -1)
    const = lambda shape: pl.BlockSpec(shape, lambda b, s: (0,) * len(shape))
    kern = functools.partial(_mixer_kernel, ts=ts, n_blocks=n_blocks, conv_width=conv_width)
    return pl.pallas_call(
        kern,
        grid=(batch, ns),
        in_specs=[pl.BlockSpec((ts, d_rnn), lambda b, s: (b * ns + s, 0)),
                  pl.BlockSpec((ts, d_rnn), lambda b, s: (b * ns + s, 1)),
                  pl.BlockSpec((ts, d_attn), lambda b, s: (b * ns + s, 0)),
                  const((conv_width, d_rnn)), const((1, d_rnn)),
                  const((n_blocks, LANES, 2 * LANES)),
                  const((1, d_rnn)), const((1, d_rnn)), const((1, d_rnn)),
                  const((1, d_attn)), const((1, d_rnn))],
        out_specs=pl.BlockSpec((ts, d_attn + d_rnn), lambda b, s: (b * ns + s, 0)),
        out_shape=jax.ShapeDtypeStruct((batch * seq, d_attn + d_rnn), BF16),
        scratch_shapes=[pltpu.VMEM((SUBLANES, d_rnn), F32),
                        pltpu.VMEM((SUBLANES, d_rnn), F32),
                        pltpu.VMEM((ts, d_rnn), F32)],
        compiler_params=_params("parallel", "arbitrary"),
        name="rglru_mixer",
    )(xg, xg, attn, conv_w, row1(conv_b), w_gates, row1(b_a), row1(b_x), row1(lam),
      row1(g_attn), row1(g_rnn))


def kernel(x, p, g_mix, w_in, conv_w, conv_b, w_rg_a, b_rg_a, w_rg_x, b_rg_x, rg_lambda,
           g_attn_out, g_rnn_out, w_out, g_ffn, w_ffn_gate, w_ffn_up, w_ffn_down, g_ple,
           w_ple_gate, w_ple_proj, g_ple_out, g_final):
    batch, seq, d_model = x.shape
    depth = w_in.shape[0]
    d_attn = g_attn_out.shape[-1]
    d_rnn = g_rnn_out.shape[-1]
    m = batch * seq

    h = x.reshape(m, d_model)
    for l in range(depth):
        u = rmsnorm(h, g_mix[l], BF16)
        w_in_l = w_in[l].astype(BF16)
        (qkv,), (w_out_l,) = matmul(u, w_in_l, BF16, PROJ_TILES, name="in_proj_qkv",
                                    n=3 * d_attn, converts=[w_out[l]])
        (xg,), (w_ple_l,) = matmul(u, w_in_l, F32, PROJ_TILES, name="in_proj_rnn",
                                   col_start=3 * d_attn, n=2 * d_rnn, converts=[w_ple_gate[l]])
        (attn,), (w_gate_l, w_up_l) = stick_breaking_attention(
            qkv, batch, seq, d_attn, converts=[w_ffn_gate[l], w_ffn_up[l]])
        w_gates = jnp.concatenate([w_rg_a[l], w_rg_x[l]], axis=-1).astype(BF16)
        mixed = mixer_epilogue(xg, attn, conv_w[l], conv_b[l], w_gates, b_rg_a[l], b_rg_x[l],
                               rg_lambda[l], g_attn_out[l], g_rnn_out[l], batch, seq)
        (h, hg, ssq), _ = matmul(mixed, w_out_l, F32, OUT_TILES, residual=h,
                                 out_gain=g_ffn[l], name="out_proj")

        (hidden,), (w_down_l,) = glu_matmul(hg, w_gate_l, w_up_l, ssq, converts=[w_ffn_down[l]])
        (h, hg, ssq), _ = matmul(hidden, w_down_l, F32, DOWN_TILES, residual=h,
                                 out_gain=g_ple[l], name="ffn_down", weight_major=True)

        h = gated_embedding_add(hg, w_ple_l, ssq, h,
                                p[l].reshape(m, -1).astype(BF16), w_ple_proj[l].astype(BF16),
                                g_ple_out[l], g_final, final_norm=(l == depth - 1))
    return h.reshape(batch, seq, d_model)
```

```python
import functools

import jax
import jax.numpy as jnp
from jax import lax
from jax.experimental import pallas as pl
from jax.experimental.pallas import tpu as pltpu

F32 = jnp.float32
BF16 = jnp.bfloat16

NORM_EPS = 1e-6
D_HEAD = 128
RGLRU_C = 8.0
LOG2_E = 1.4426950408889634
MIN_LOG2_WEIGHT = -150.0
F32_TINY = 1.1754943508222875e-38

LANES = 128
SUBLANES = 8
VMEM_LIMIT_BYTES = 60 * 1024 * 1024

PROJ_TILES = (1024, 1024)
OUT_TILES = (1024, 1024)
GLU_TILES = (2048, 256)
DOWN_TILES = (512, 1024)
PLE_TILES = (512, 512)
W_RING = 3
NORM_ROWS = 256
ATTN_ROWS = 256
ATTN_HEADS = 8
ATTN_SPLIT_16THS = 11
SCAN_ROWS = 256
EPILOGUE_ROWS = 256


def _params(*semantics):
    return pltpu.CompilerParams(dimension_semantics=semantics,
                                vmem_limit_bytes=VMEM_LIMIT_BYTES)


def _tile(n, want):
    t = min(n, want)
    assert n % t == 0, (n, want)
    return t


def _rms(x, g):
    y = x * lax.rsqrt(jnp.mean(x * x, axis=-1, keepdims=True) + NORM_EPS)
    return y * g


def _rmsnorm_kernel(x_ref, g_ref, o_ref):
    o_ref[...] = _rms(x_ref[...], g_ref[...]).astype(o_ref.dtype)


def rmsnorm(x, g, out_dtype):
    m, d = x.shape
    tm = _tile(m, NORM_ROWS)
    return pl.pallas_call(
        _rmsnorm_kernel,
        grid=(m // tm,),
        in_specs=[pl.BlockSpec((tm, d), lambda i: (i, 0)),
                  pl.BlockSpec((1, d), lambda i: (0, 0))],
        out_specs=pl.BlockSpec((tm, d), lambda i: (i, 0)),
        out_shape=jax.ShapeDtypeStruct((m, d), out_dtype),
        compiler_params=_params("parallel"),
        name="rmsnorm",
    )(x, g.reshape(1, d))


def _row_chunks(rows):
    ch = min(rows, EPILOGUE_ROWS)
    assert rows % ch == 0
    return [pl.ds(c * ch, ch) for c in range(rows // ch)]


def _inv_rms(ssq_ref, rows, d):
    parts = ssq_ref.shape[1] // LANES
    total = ssq_ref[rows, 0:1]
    for p in range(1, parts):
        total = total + ssq_ref[rows, p * LANES:p * LANES + 1]
    return lax.rsqrt(total * (1.0 / d) + NORM_EPS)


def _emit_scaled(h, rows, g_ref, hg_ref, ssq_ref):
    hg_ref[rows, :] = (h * g_ref[...]).astype(hg_ref.dtype)
    part = jnp.sum(h * h, axis=1, keepdims=True)
    ssq_ref[rows, :] = jnp.broadcast_to(part, (h.shape[0], ssq_ref.shape[1]))


class _Side:
    def __init__(self, w, n_steps, linear_step):
        rows, cols = w.shape
        units = rows // (2 * SUBLANES)
        assert units * 2 * SUBLANES == rows
        n_chunks = max(c for c in range(1, min(units, n_steps) + 1) if units % c == 0)
        chunk = rows // n_chunks
        index = lambda *g: (jnp.minimum(linear_step(*g), n_chunks - 1), 0)
        self.w = w
        self.in_spec = pl.BlockSpec((chunk, cols), index)
        self.out_spec = pl.BlockSpec((chunk, cols), index)
        self.out_shape = jax.ShapeDtypeStruct((rows, cols), BF16)


def _with_sides(body, n_in, n_out, n_side):
    if n_side == 0:
        return body

    def wrapped(*refs):
        side_in = refs[n_in:n_in + n_side]
        o0 = n_in + n_side
        side_out = refs[o0 + n_out:o0 + n_out + n_side]
        for src, dst in zip(side_in, side_out):
            dst[...] = src[...].astype(dst.dtype)
        body(*refs[:n_in], *refs[o0:o0 + n_out], *refs[o0 + n_out + n_side:])

    return wrapped


def _call(body, grid, in_specs, out_specs, out_shapes, args, sides, semantics, name,
          scratch_shapes=()):
    n_in, n_out = len(in_specs), len(out_specs)
    outs = pl.pallas_call(
        _with_sides(body, n_in, n_out, len(sides)),
        grid=grid,
        in_specs=list(in_specs) + [s.in_spec for s in sides],
        out_specs=list(out_specs) + [s.out_spec for s in sides],
        out_shape=list(out_shapes) + [s.out_shape for s in sides],
        scratch_shapes=list(scratch_shapes),
        compiler_params=_params(*semantics),
        name=name,
    )(*args, *[s.w for s in sides])
    return outs[:n_out], outs[n_out:]


def _mm_kernel(*refs, has_res, d_scale, has_gain):
    refs = list(refs)
    a_ref, w_ref = refs.pop(0), refs.pop(0)
    r_ref = refs.pop(0) if has_res else None
    s_ref = refs.pop(0) if d_scale else None
    g_ref = refs.pop(0) if has_gain else None
    o_ref = refs.pop(0)
    if has_gain:
        hg_ref, ssq_ref = refs
    for rows in _row_chunks(a_ref.shape[0]):
        acc = jnp.dot(a_ref[rows, :], w_ref[...], preferred_element_type=F32)
        if d_scale:
            acc = acc * _inv_rms(s_ref, rows, d_scale)
        if has_res:
            acc = r_ref[rows, :] + acc
        o_ref[rows, :] = acc.astype(o_ref.dtype)
        if has_gain:
            _emit_scaled(acc, rows, g_ref, hg_ref, ssq_ref)


def matmul(a, w, out_dtype, tiles, *, name, residual=None, col_start=0, n=None,
           in_ssq=None, out_gain=None, converts=(), weight_major=False):
    m, k = a.shape
    n = w.shape[1] if n is None else n
    tm, tn = _tile(m, tiles[0]), _tile(n, tiles[1])
    assert col_start % tn == 0
    j0 = col_start // tn
    gi, gj = m // tm, n // tn
    if weight_major:
        grid = (gj, gi)
        ij = lambda f: (lambda j, i: f(i, j))
        w_mode = dict(pipeline_mode=pl.Buffered(1))
        step = lambda j, i: j * gi + i
    else:
        grid = (gi, gj)
        ij = lambda f: f
        w_mode = {}
        step = lambda i, j: i * gj + j
    in_specs = [pl.BlockSpec((tm, k), ij(lambda i, j: (i, 0))),
                pl.BlockSpec((k, tn), ij(lambda i, j: (0, j0 + j)), **w_mode)]
    args = [a, w]
    if residual is not None:
        in_specs.append(pl.BlockSpec((tm, tn), ij(lambda i, j: (i, j))))
        args.append(residual)
    if in_ssq is not None:
        in_specs.append(pl.BlockSpec((tm, in_ssq.shape[1]), ij(lambda i, j: (i, 0))))
        args.append(in_ssq)
    out_specs = [pl.BlockSpec((tm, tn), ij(lambda i, j: (i, j)))]
    out_shapes = [jax.ShapeDtypeStruct((m, n), out_dtype)]
    if out_gain is not None:
        in_specs.append(pl.BlockSpec((1, tn), ij(lambda i, j: (0, j))))
        args.append(out_gain.reshape(1, n))
        out_specs += [pl.BlockSpec((tm, tn), ij(lambda i, j: (i, j))),
                      pl.BlockSpec((tm, LANES), ij(lambda i, j: (i, j)))]
        out_shapes += [jax.ShapeDtypeStruct((m, n), BF16),
                       jax.ShapeDtypeStruct((m, gj * LANES), F32)]
    body = functools.partial(_mm_kernel, has_res=residual is not None,
                             d_scale=k if in_ssq is not None else 0,
                             has_gain=out_gain is not None)
    sides = [_Side(c, gi * gj, step) for c in converts]
    return _call(body, grid, in_specs, out_specs, out_shapes, args, sides,
                 ("arbitrary", "arbitrary"), name)


def _glu_kernel(a_ref, wg_ref, wu_ref, s_ref, o_ref, *, d_scale):
    for rows in _row_chunks(a_ref.shape[0]):
        a = a_ref[rows, :]
        inv = _inv_rms(s_ref, rows, d_scale)
        g = jnp.dot(a, wg_ref[...], preferred_element_type=F32) * inv
        u = jnp.dot(a, wu_ref[...], preferred_element_type=F32) * inv
        o_ref[rows, :] = (g * jax.nn.sigmoid(g) * u).astype(o_ref.dtype)


def glu_matmul(a, wg, wu, in_ssq, converts=()):
    m, k = a.shape
    n = wg.shape[1]
    tm, tn = _tile(m, GLU_TILES[0]), _tile(n, GLU_TILES[1])
    gj = n // tn
    sides = [_Side(c, (m // tm) * gj, lambda i, j: i * gj + j) for c in converts]
    return _call(
        functools.partial(_glu_kernel, d_scale=k), (m // tm, gj),
        [pl.BlockSpec((tm, k), lambda i, j: (i, 0)),
         pl.BlockSpec((k, tn), lambda i, j: (0, j)),
         pl.BlockSpec((k, tn), lambda i, j: (0, j)),
         pl.BlockSpec((tm, in_ssq.shape[1]), lambda i, j: (i, 0))],
        [pl.BlockSpec((tm, tn), lambda i, j: (i, j))],
        [jax.ShapeDtypeStruct((m, n), BF16)],
        [a, wg, wu, in_ssq], sides, ("parallel", "arbitrary"), "ffn_gate_up")


def _ple_kernel(a_ref, w_hbm, s_ref, h_ref, p_ref, wp_ref, gp_ref, gf_ref, o_ref, pe_ref, ssq_ref,
                w_buf, w_sem, *, tn, d_scale, final_norm):
    i, j = pl.program_id(0), pl.program_id(1)
    gj = pl.num_programs(1)
    steps = pl.num_programs(0) * gj
    s = i * gj + j
    tm, n = o_ref.shape

    def w_copy(step):
        col = pl.multiple_of((step % gj) * tn, tn)
        slot = step % W_RING
        return pltpu.make_async_copy(w_hbm.at[:, pl.ds(col, tn)], w_buf.at[slot],
                                     w_sem.at[slot])

    @pl.when(s == 0)
    def _():
        for ahead in range(W_RING - 1):
            @pl.when(ahead < steps)
            def _():
                w_copy(s + ahead).start()

    @pl.when(s + W_RING - 1 < steps)
    def _():
        w_copy(s + W_RING - 1).start()

    w_copy(s).wait()
    w_ref = w_buf.at[s % W_RING]

    @pl.when(j == 0)
    def _():
        pe_ref[...] = _rms(jnp.dot(p_ref[...], wp_ref[...], preferred_element_type=F32),
                           gp_ref[...])
        ssq_ref[...] = jnp.zeros_like(ssq_ref)

    cols = pl.ds(pl.multiple_of(j * tn, tn), tn)
    for rows in _row_chunks(tm):
        z = (jnp.dot(a_ref[rows, :], w_ref[...], preferred_element_type=F32)
             * _inv_rms(s_ref, rows, d_scale))
        out = h_ref[rows, :] + jax.nn.sigmoid(z) * pe_ref[rows, cols]
        o_ref[rows, cols] = out
        if final_norm:
            part = jnp.sum(out * out, axis=1, keepdims=True)
            ssq_ref[rows, :] += jnp.broadcast_to(part, (out.shape[0], LANES))

    if final_norm:
        @pl.when(j == pl.num_programs(1) - 1)
        def _():
            step = 2 * SUBLANES
            for r0 in range(0, tm, step):
                rows = pl.ds(r0, step)
                o_ref[rows, :] = o_ref[rows, :] * _inv_rms(ssq_ref, rows, n) * gf_ref[...]


def gated_embedding_add(a, w, in_ssq, h, p, wp, gp, gf, final_norm):
    m, k = a.shape
    n = w.shape[1]
    kp = p.shape[1]
    tm, tn = _tile(m, PLE_TILES[0]), _tile(n, PLE_TILES[1])
    kern = functools.partial(_ple_kernel, tn=tn, d_scale=k, final_norm=final_norm)
    return pl.pallas_call(
        kern,
        grid=(m // tm, n // tn),
        in_specs=[pl.BlockSpec((tm, k), lambda i, j: (i, 0)),
                  pl.BlockSpec(memory_space=pl.ANY),
                  pl.BlockSpec((tm, in_ssq.shape[1]), lambda i, j: (i, 0)),
                  pl.BlockSpec((tm, tn), lambda i, j: (i, j)),
                  pl.BlockSpec((tm, kp), lambda i, j: (i, 0)),
                  pl.BlockSpec((kp, n), lambda i, j: (0, 0)),
                  pl.BlockSpec((1, n), lambda i, j: (0, 0)),
                  pl.BlockSpec((1, n), lambda i, j: (0, 0))],
        out_specs=pl.BlockSpec((tm, n), lambda i, j: (i, 0)),
        out_shape=jax.ShapeDtypeStruct((m, n), F32),
        scratch_shapes=[pltpu.VMEM((tm, n), F32), pltpu.VMEM((tm, LANES), F32),
                        pltpu.VMEM((W_RING, k, tn), BF16), pltpu.SemaphoreType.DMA((W_RING,))],
        compiler_params=_params("arbitrary", "arbitrary"),
        name="ple_gate",
    )(a, w, in_ssq, h, p, wp, gp.reshape(1, n), gf.reshape(1, n))


def _suffix_sum_matrix(n):
    r = lax.broadcasted_iota(jnp.int32, (2 * n, n), 0) % n
    c = lax.broadcasted_iota(jnp.int32, (2 * n, n), 1)
    return jnp.where(r >= c, 1.0, 0.0).astype(BF16)


def _attn_kernel(q_ref, k_ref, v_ref, t_ref, o_ref, acc_ref, carry_ref, *, tq, heads, scale):
    qi = pl.program_id(2)
    groups = tq // LANES
    sign_bit = jnp.uint32(0x80000000)
    split = tq * ATTN_SPLIT_16THS // 16 // (2 * SUBLANES) * (2 * SUBLANES)
    assert 0 < split < tq

    def all_heads(kstart, masked, nrows=tq):
        hs = range(heads)
        cols = [slice(h * D_HEAD, (h + 1) * D_HEAD) for h in hs]
        if masked:
            row = lax.broadcasted_iota(jnp.int32, (tq, tq), 0)
            col = lax.broadcasted_iota(jnp.int32, (tq, tq), 1)
            causal = col < row
        nz = [lax.dot_general(q_ref[0:nrows, cols[h]], k_ref[pl.ds(kstart, tq), cols[h]],
                              (((1,), (1,)), ((), ())), preferred_element_type=F32)
              * (-scale * LOG2_E) for h in hs]
        sums = []
        for h in hs:
            neg_abs = lax.bitcast_convert_type(
                lax.bitcast_convert_type(nz[h], jnp.uint32) | sign_bit, F32)
            log_keep = jnp.minimum(nz[h], 0.0) - jnp.log(1.0 + jnp.exp2(neg_abs)) * LOG2_E
            if masked:
                log_keep = jnp.where(causal, log_keep, 0.0)
            hi = log_keep.astype(BF16)
            lo = (log_keep - hi.astype(F32)).astype(BF16)
            sums.append(jnp.dot(jnp.concatenate([hi, lo], axis=1), t_ref[...],
                                preferred_element_type=F32))
        tops = [None, None]
        for h in hs:
            x = sums[h] - nz[h]
            total = jnp.broadcast_to(sums[h][:, 0:1], (nrows, LANES))
            vc = v_ref[pl.ds(kstart, tq), cols[h]]
            if masked:
                w = jnp.where(causal, jnp.exp2(x), 0.0)
                acc_ref[h] = jnp.dot(w.astype(BF16), vc, preferred_element_type=F32)
                carry = total
            else:
                carry = carry_ref[h, 0:nrows]
                w = jnp.exp2(x + jnp.concatenate([carry] * groups, axis=1))
                acc_ref[h, 0:nrows] += jnp.dot(w.astype(BF16), vc, preferred_element_type=F32)
                carry = carry + total
            carry_ref[h, 0:nrows] = carry
            for part, rows in enumerate((slice(0, split), slice(split, nrows))):
                if rows.start < rows.stop:
                    m = jnp.max(carry[rows])
                    tops[part] = m if tops[part] is None else jnp.maximum(tops[part], m)
        return tops

    _, top_hi = all_heads(pl.multiple_of(qi * tq, tq), True)
    top_lo = jnp.float32(0.0)

    def more(c):
        j, top_lo, top_hi = c
        return jnp.logical_and(j < qi, jnp.maximum(top_lo, top_hi) >= MIN_LOG2_WEIGHT)

    def body(c):
        j, _, top_hi = c
        kstart = pl.multiple_of((qi - 1 - j) * tq, tq)
        top_lo, top_hi = lax.cond(
            top_hi >= MIN_LOG2_WEIGHT,
            lambda: tuple(all_heads(kstart, False)),
            lambda: (all_heads(kstart, False, nrows=split)[0], top_hi))
        return j + 1, top_lo, top_hi

    lax.while_loop(more, body, (jnp.int32(0), top_lo, top_hi))

    for h in range(heads):
        o_ref[:, h * D_HEAD:(h + 1) * D_HEAD] = acc_ref[h]


def stick_breaking_attention(qkv, batch, seq, d_attn, converts=()):
    n_heads = d_attn // D_HEAD
    heads = min(ATTN_HEADS, n_heads)
    assert n_heads % heads == 0
    width = heads * D_HEAD
    pairs = d_attn // width
    tq = _tile(seq, ATTN_ROWS)
    nq = seq // tq
    kern = functools.partial(_attn_kernel, tq=tq, heads=heads, scale=D_HEAD ** -0.5)
    sides = [_Side(c, batch * pairs * nq, lambda b, p, i: (b * pairs + p) * nq + i)
             for c in converts]
    return _call(
        kern, (batch, pairs, nq),
        [pl.BlockSpec((tq, width), lambda b, p, i: (b * nq + i, p)),
         pl.BlockSpec((seq, width), lambda b, p, i: (b, pairs + p)),
         pl.BlockSpec((seq, width), lambda b, p, i: (b, 2 * pairs + p)),
         pl.BlockSpec((2 * tq, tq), lambda b, p, i: (0, 0))],
        [pl.BlockSpec((tq, width), lambda b, p, i: (b * nq + i, p))],
        [jax.ShapeDtypeStruct((batch * seq, d_attn), F32)],
        [qkv, qkv, qkv, _suffix_sum_matrix(tq)], sides,
        ("parallel", "parallel", "arbitrary"), "stick_breaking_attention",
        scratch_shapes=[pltpu.VMEM((heads, tq, D_HEAD), F32),
                        pltpu.VMEM((heads, tq, LANES), F32)])


def _softplus(x):
    return jnp.maximum(x, 0.0) + jnp.log1p(jnp.exp(-jnp.abs(x)))


def _mixer_kernel(xr_ref, gr_ref, attn_ref, cw_ref, cb_ref, wg_ref, ba_ref, bx_ref, lam_ref,
                  ga_ref, gn_ref, o_ref, tail_ref, h_ref, rec_ref, *, ts, n_blocks, conv_width):
    @pl.when(pl.program_id(1) == 0)
    def _():
        tail_ref[...] = jnp.zeros_like(tail_ref)
        h_ref[...] = jnp.zeros_like(h_ref)

    row8 = lax.broadcasted_iota(jnp.int32, (SUBLANES, LANES), 0)
    sub3 = lax.broadcasted_iota(jnp.int32, (ts // SUBLANES, SUBLANES, LANES), 1)

    def block(n, ssq):
        lanes = pl.ds(pl.multiple_of(n * LANES, LANES), LANES)
        x = xr_ref[:, lanes]
        tail = tail_ref[:, lanes]
        assert conv_width - 1 < SUBLANES
        y = cb_ref[:, lanes] + cw_ref[conv_width - 1:conv_width, lanes] * x
        x3 = x.reshape(ts // SUBLANES, SUBLANES, LANES)
        for d in range(1, conv_width):
            rot = pltpu.roll(x3, d, axis=1)
            prev = jnp.concatenate([pltpu.roll(tail, d, axis=0)[None], rot[:-1]], axis=0)
            xs = jnp.where(sub3 >= d, rot, prev).reshape(ts, LANES)
            y = y + cw_ref[conv_width - 1 - d:conv_width - d, lanes] * xs
        tail_ref[:, lanes] = x[ts - SUBLANES:]

        gates = jnp.dot(y.astype(BF16), wg_ref[n], preferred_element_type=F32)
        r = jax.nn.sigmoid(gates[:, :LANES] + ba_ref[:, lanes])
        i = jax.nn.sigmoid(gates[:, LANES:] + bx_ref[:, lanes])
        log_a = -RGLRU_C * r * _softplus(-lam_ref[:, lanes])
        a = jnp.exp(log_a)
        v = -jnp.tanh(log_a) * (a * a + 1.0)
        u = v * lax.rsqrt(jnp.maximum(v, F32_TINY)) * (i * y)

        a = a.reshape(ts // SUBLANES, SUBLANES, LANES)
        u = u.reshape(ts // SUBLANES, SUBLANES, LANES)
        for d in (1, 2, 4):
            keep = sub3 >= d
            a_prev = jnp.where(keep, pltpu.roll(a, d, axis=1), 1.0)
            u_prev = jnp.where(keep, pltpu.roll(u, d, axis=1), 0.0)
            u = a * u_prev + u
            a = a * a_prev
        a = a.reshape(ts, LANES)
        u = u.reshape(ts, LANES)
        h_prev = h_ref[:, lanes]
        hs = []
        for g in range(ts // SUBLANES):
            rows = slice(g * SUBLANES, (g + 1) * SUBLANES)
            hg = a[rows] * h_prev + u[rows]
            hs.append(hg)
            h_prev = jnp.broadcast_to(hg[SUBLANES - 1:SUBLANES], (SUBLANES, LANES))
        h_ref[:, lanes] = h_prev
        rec = jnp.concatenate(hs, axis=0) * jax.nn.gelu(gr_ref[:, lanes], approximate=True)
        rec_ref[:, lanes] = rec
        return ssq + jnp.sum(rec * rec, axis=1, keepdims=True)

    ssq = lax.fori_loop(0, n_blocks, block, jnp.zeros((ts, 1), F32),
                        unroll=2 if n_blocks % 2 == 0 else 1)
    d_attn = attn_ref.shape[1]
    d_rnn = n_blocks * LANES
    inv = lax.rsqrt(ssq / d_rnn + NORM_EPS)
    step = 2 * SUBLANES
    for r0 in range(0, ts, step):
        rows = pl.ds(r0, step)
        o_ref[rows, :d_attn] = _rms(attn_ref[rows, :], ga_ref[...]).astype(o_ref.dtype)
        o_ref[rows, d_attn:] = (rec_ref[rows, :] * inv[r0:r0 + step]
                                * gn_ref[...]).astype(o_ref.dtype)


def mixer_epilogue(xg, attn, conv_w, conv_b, w_gates, b_a, b_x, lam, g_attn, g_rnn, batch, seq):
    d_rnn = xg.shape[1] // 2
    d_attn = attn.shape[1]
    n_blocks = d_rnn // LANES
    assert w_gates.shape == (n_blocks, LANES, 2 * LANES)
    conv_width = conv_w.shape[0]
    ts = _tile(seq, SCAN_ROWS)
    ns = seq // ts
    row1 = lambda v: v.reshape(1, -1)
    const = lambda shape: pl.BlockSpec(shape, lambda b, s: (0,) * len(shape))
    kern = functools.partial(_mixer_kernel, ts=ts, n_blocks=n_blocks, conv_width=conv_width)
    return pl.pallas_call(
        kern,
        grid=(batch, ns),
        in_specs=[pl.BlockSpec((ts, d_rnn), lambda b, s: (b * ns + s, 0)),
                  pl.BlockSpec((ts, d_rnn), lambda b, s: (b * ns + s, 1)),
                  pl.BlockSpec((ts, d_attn), lambda b, s: (b * ns + s, 0)),
                  const((conv_width, d_rnn)), const((1, d_rnn)),
                  const((n_blocks, LANES, 2 * LANES)),
                  const((1, d_rnn)), const((1, d_rnn)), const((1, d_rnn)),
                  const((1, d_attn)), const((1, d_rnn))],
        out_specs=pl.BlockSpec((ts, d_attn + d_rnn), lambda b, s: (b * ns + s, 0)),
        out_shape=jax.ShapeDtypeStruct((batch * seq, d_attn + d_rnn), BF16),
        scratch_shapes=[pltpu.VMEM((SUBLANES, d_rnn), F32),
                        pltpu.VMEM((SUBLANES, d_rnn), F32),
                        pltpu.VMEM((ts, d_rnn), F32)],
        compiler_params=_params("parallel", "arbitrary"),
        name="rglru_mixer",
    )(xg, xg, attn, conv_w, row1(conv_b), w_gates, row1(b_a), row1(b_x), row1(lam),
      row1(g_attn), row1(g_rnn))


def kernel(x, p, g_mix, w_in, conv_w, conv_b, w_rg_a, b_rg_a, w_rg_x, b_rg_x, rg_lambda,
           g_attn_out, g_rnn_out, w_out, g_ffn, w_ffn_gate, w_ffn_up, w_ffn_down, g_ple,
           w_ple_gate, w_ple_proj, g_ple_out, g_final):
    batch, seq, d_model = x.shape
    depth = w_in.shape[0]
    d_attn = g_attn_out.shape[-1]
    d_rnn = g_rnn_out.shape[-1]
    m = batch * seq

    h = x.reshape(m, d_model)
    for l in range(depth):
        u = rmsnorm(h, g_mix[l], BF16)
        w_in_l = w_in[l].astype(BF16)
        (qkv,), (w_out_l,) = matmul(u, w_in_l, BF16, PROJ_TILES, name="in_proj_qkv",
                                    n=3 * d_attn, converts=[w_out[l]])
        (xg,), (w_ple_l,) = matmul(u, w_in_l, F32, PROJ_TILES, name="in_proj_rnn",
                                   col_start=3 * d_attn, n=2 * d_rnn, converts=[w_ple_gate[l]])
        (attn,), (w_gate_l, w_up_l) = stick_breaking_attention(
            qkv, batch, seq, d_attn, converts=[w_ffn_gate[l], w_ffn_up[l]])
        w_gates = jnp.concatenate([w_rg_a[l], w_rg_x[l]], axis=-1).astype(BF16)
        mixed = mixer_epilogue(xg, attn, conv_w[l], conv_b[l], w_gates, b_rg_a[l], b_rg_x[l],
                               rg_lambda[l], g_attn_out[l], g_rnn_out[l], batch, seq)
        (h, hg, ssq), _ = matmul(mixed, w_out_l, F32, OUT_TILES, residual=h,
                                 out_gain=g_ffn[l], name="out_proj")

        (hidden,), (w_down_l,) = glu_matmul(hg, w_gate_l, w_up_l, ssq, converts=[w_ffn_down[l]])
        (h, hg, ssq), _ = matmul(hidden, w_down_l, F32, DOWN_TILES, residual=h,
                                 out_gain=g_ple[l], name="ffn_down", weight_major=True)

        h = gated_embedding_add(hg, w_ple_l, ssq, h,
                                p[l].reshape(m, -1).astype(BF16), w_ple_proj[l].astype(BF16),
                                g_ple_out[l], g_final, final_norm=(l == depth - 1))
    return h.reshape(batch, seq, d_model)
```
